```python
import math
import jax, jax.numpy as jnp
from jax import lax
import numpy as np

D_MODEL = 2048
BATCH = 16
SEQ = 256
DEPTH = 4
DEC_BATCH = 2
DEC_SEQ = 1024
PAST_LEN = 512

GRID_W = 64
N_EVEN = (DEPTH + 1) // 2
N_ODD = DEPTH // 2
FOURIER_GROUPS = 4
FOURIER_GROUP_DIM = 128
FOURIER_DIM = FOURIER_GROUPS * FOURIER_GROUP_DIM
MLA_HEADS = 12
Q_RANK = 768
KV_RANK = 256
NOPE_DIM = 128
ROPE_DIM = 64
V_DIM = 128
QK_DIM = NOPE_DIM + ROPE_DIM
ATTN_SCALE = QK_DIM ** -0.5
ROPE_THETA = 10000.0
Q_BLOCK = 128
MIX_IN = FOURIER_DIM + Q_RANK + KV_RANK + ROPE_DIM
MIX_OUT = FOURIER_DIM + MLA_HEADS * V_DIM
SSM_INNER = 2 * D_MODEL
SSM_HEAD_DIM = 64
SSM_HEADS = SSM_INNER // SSM_HEAD_DIM
SSM_GROUPS = 8
SSM_STATE = 128
SSM_CONV = 5
SSM_CHUNK = 128
SSM_CONV_DIM = SSM_INNER + 2 * SSM_GROUPS * SSM_STATE
SSM_IN = SSM_INNER + SSM_CONV_DIM + 2 * SSM_HEADS
FFN_DIM = 5632
N_EXPERTS = 8
TOP_K = 2
EXPERT_DIM = 4096
ALPHA = (2 * DEPTH) ** 0.25
BETA = (8 * DEPTH) ** -0.25
EPS = 1e-5

kernel_name = 'hybrid_fourier_mla_ssd_diffusion_step'


def layer_norm(x, g, b):
    xf = x.astype(jnp.float32)
    mu = jnp.mean(xf, -1, keepdims=True)
    var = jnp.mean(jnp.square(xf - mu), -1, keepdims=True)
    return ((xf - mu) * lax.rsqrt(var + EPS)).astype(x.dtype) * g + b


def rms_norm(x, w):
    xf = x.astype(jnp.float32)
    return (xf * lax.rsqrt(jnp.mean(jnp.square(xf), -1, keepdims=True) + EPS)).astype(x.dtype) * w


def swiglu(t, w_gate, w_up, w_down):
    return (jax.nn.silu(t @ w_gate) * (t @ w_up)) @ w_down


def rope_angles(length):
    rows = length // GRID_W
    row = jnp.repeat(jnp.arange(rows, dtype=jnp.float32), GRID_W)
    col = jnp.tile(jnp.arange(GRID_W, dtype=jnp.float32), rows)
    inv = ROPE_THETA ** (-jnp.arange(ROPE_DIM // 4, dtype=jnp.float32) * 2.0 / (ROPE_DIM // 2))
    return jnp.stack([row[:, None] * inv, col[:, None] * inv], axis=1)


def rope2d(x, ang):
    shp = x.shape
    xr = x.reshape(*shp[:-1], 2, 2, ROPE_DIM // 4)
    x1, x2 = xr[..., 0, :], xr[..., 1, :]
    cos, sin = jnp.cos(ang).astype(x.dtype), jnp.sin(ang).astype(x.dtype)
    if x.ndim == 4:
        cos, sin = cos[:, None], sin[:, None]
    return jnp.stack([x1 * cos - x2 * sin, x1 * sin + x2 * cos], axis=-2).reshape(shp)


def mla_attend(q_nope, q_rope, k_nope, k_rope, v):
    b, lq, h, _ = q_nope.shape
    nb = lq // Q_BLOCK

    def blocks(t):
        return jnp.moveaxis(t.reshape(b, nb, Q_BLOCK, *t.shape[2:]), 1, 0)

    def one_block(qs):
        qn, qr = qs
        s = (jnp.einsum('bqhd,bkhd->bhqk', qn, k_nope)
             + jnp.einsum('bqhr,bkr->bhqk', qr, k_rope)).astype(jnp.float32)
        p = jax.nn.softmax(s * ATTN_SCALE, axis=-1).astype(v.dtype)
        return jnp.einsum('bhqk,bkhd->bqhd', p, v)

    o = lax.map(one_block, (blocks(q_nope), blocks(q_rope)))
    return jnp.moveaxis(o, 0, 1).reshape(b, lq, h * V_DIM)


def fourier_mla_mixer(u, w_in, q_norm_w, w_q_b, kv_norm_w, w_kv_b, w_out, ang, ctx):
    b, l, _ = u.shape
    f_in, q_c, kv_c, k_r = jnp.split(u @ w_in, [FOURIER_DIM, FOURIER_DIM + Q_RANK,
                                                FOURIER_DIM + Q_RANK + KV_RANK], axis=-1)
    f = f_in.reshape(b, l, FOURIER_GROUPS, FOURIER_GROUP_DIM).astype(jnp.float32)
    f = jnp.fft.fft2(f, axes=(1, 3), norm='ortho').real.astype(u.dtype).reshape(b, l, FOURIER_DIM)
    q = (rms_norm(q_c, q_norm_w) @ w_q_b).reshape(b, l, MLA_HEADS, QK_DIM)
    q_nope, q_rope = q[..., :NOPE_DIM], q[..., NOPE_DIM:]
    c_kv = rms_norm(kv_c, kv_norm_w)
    k_rope = k_r
    if ang is not None:
        q_rope = rope2d(q_rope, ang)
        k_rope = rope2d(k_r, ang)
    kv = (c_kv @ w_kv_b).reshape(b, l, MLA_HEADS, NOPE_DIM + V_DIM)
    k_nope, v = kv[..., :NOPE_DIM], kv[..., NOPE_DIM:]
    if ctx is not None:
        ctx_ckv, ctx_kr = ctx
        kvc = (ctx_ckv @ w_kv_b).reshape(b, -1, MLA_HEADS, NOPE_DIM + V_DIM)
        k_nope = jnp.concatenate([kvc[..., :NOPE_DIM], k_nope], axis=1)
        v = jnp.concatenate([kvc[..., NOPE_DIM:], v], axis=1)
        k_rope = jnp.concatenate([ctx_kr, k_rope], axis=1)
    o = mla_attend(q_nope, q_rope, k_nope, k_rope, v)
    out = jnp.concatenate([f, o], axis=-1) @ w_out
    return out, c_kv, k_r


def segsum(a):
    t = a.shape[-1]
    cs = jnp.cumsum(a, axis=-1)
    d = cs[..., :, None] - cs[..., None, :]
    return jnp.where(jnp.tril(jnp.ones((t, t), dtype=bool)), d, -jnp.inf)


def ssd_scan(x, dt, a_neg, bm, cm, h0):
    b, l, nh, hp = x.shape
    g, n = bm.shape[2], bm.shape[3]
    r = nh // g
    nc = l // SSM_CHUNK
    xd = (x * dt[..., None]).reshape(b, nc, SSM_CHUNK, g, r, hp)
    a = jnp.transpose((dt * a_neg).reshape(b, nc, SSM_CHUNK, g, r), (0, 1, 3, 4, 2))
    bc = bm.reshape(b, nc, SSM_CHUNK, g, n)
    cc = cm.reshape(b, nc, SSM_CHUNK, g, n)
    a_cum = jnp.cumsum(a, axis=-1)
    cb = jnp.einsum('bclgn,bcsgn->bcgls', cc, bc)
    m = cb[:, :, :, None] * jnp.exp(segsum(a))
    y_diag = jnp.einsum('bcgrls,bcsgrp->bclgrp', m, xd)
    decay_s = jnp.exp(a_cum[..., -1:] - a_cum)
    states = jnp.einsum('bclgn,bcgrl,bclgrp->bcgrpn', bc, decay_s, xd)
    states = jnp.concatenate([h0.reshape(b, 1, g, r, hp, n), states], axis=1)
    a_last = jnp.pad(jnp.moveaxis(a_cum[..., -1], 1, -1), ((0, 0), (0, 0), (0, 0), (1, 0)))
    new = jnp.einsum('bgrzc,bcgrpn->bzgrpn', jnp.exp(segsum(a_last)), states)
    prev, final = new[:, :-1], new[:, -1]
    y_off = jnp.einsum('bclgn,bcgrpn,bcgrl->bclgrp', cc, prev, jnp.exp(a_cum))
    y = (y_diag + y_off).reshape(b, l, nh, hp)
    return y, final.reshape(b, nh, hp, n)


def centred_dwconv(x, w, bias):
    y = lax.conv_general_dilated(x, w[:, None, :], window_strides=(1,),
                                 padding=[(SSM_CONV // 2, SSM_CONV // 2)],
                                 dimension_numbers=('NWC', 'WIO', 'NWC'),
                                 feature_group_count=x.shape[-1])
    return y + bias


def ssd_mixer(u, w_in, conv_w, conv_b, dt_bias, a_log, d_skip, norm_w, w_out, h0_f, h0_b):
    b, l, _ = u.shape
    z, xbc, dt = jnp.split(u @ w_in, [SSM_INNER, SSM_INNER + SSM_CONV_DIM], axis=-1)
    xbc = jax.nn.silu(centred_dwconv(xbc, conv_w, conv_b))
    xs, bm, cm = jnp.split(xbc, [SSM_INNER, SSM_INNER + SSM_GROUPS * SSM_STATE], axis=-1)
    xs = xs.reshape(b, l, SSM_HEADS, SSM_HEAD_DIM).astype(jnp.float32)
    bm = bm.reshape(b, l, SSM_GROUPS, SSM_STATE).astype(jnp.float32)
    cm = cm.reshape(b, l, SSM_GROUPS, SSM_STATE).astype(jnp.float32)
    dt = jax.nn.softplus(dt.astype(jnp.float32).reshape(b, l, 2, SSM_HEADS) + dt_bias.astype(jnp.float32))
    a_neg = -jnp.exp(a_log.astype(jnp.float32))
    y_f, hf = ssd_scan(xs, dt[:, :, 0], a_neg[0], bm, cm, h0_f.astype(jnp.float32))
    fl = lambda t: jnp.flip(t, axis=1)
    y_b, hb = ssd_scan(fl(xs), fl(dt[:, :, 1]), a_neg[1], fl(bm), fl(cm), h0_b.astype(jnp.float32))
    y = y_f + fl(y_b) + d_skip.astype(jnp.float32)[:, None] * xs
    y = y.reshape(b, l, SSM_INNER).astype(u.dtype) * jax.nn.silu(z)
    return rms_norm(y, norm_w) @ w_out, hf, hb


def moe_swiglu(u, router, w_gate, w_up, w_down):
    b, l, d = u.shape
    t = u.reshape(b * l, d)
    logits = (t @ router).astype(jnp.float32)
    top_v, top_i = lax.top_k(logits, TOP_K)
    probs = jax.nn.softmax(top_v, axis=-1)
    gates = jnp.einsum('tk,tke->te', probs,
                       jax.nn.one_hot(top_i, N_EXPERTS, dtype=jnp.float32)).astype(u.dtype)
    out = jnp.zeros_like(t)
    for e in range(N_EXPERTS):
        out = out + gates[:, e:e + 1] * swiglu(t, w_gate[e], w_up[e], w_down[e])
    return out.reshape(b, l, d)


def trunk(x, cond, p, ang, ctx):
    b = x.shape[0]
    ckvs, krs, sfs, sbs = [], [], [], []
    for i in range(DEPTH):
        j = i // 2
        mod = jax.nn.silu(cond) @ p['ada_w'][i] + p['ada_b'][i]
        sh_m, sc_m, g_m, sh_f, sc_f, g_f = [m[:, None, :] for m in jnp.split(mod, 6, axis=-1)]
        u = x * (1 + sc_m) + sh_m
        if i % 2 == 0:
            layer_ctx = None if ctx is None else (ctx[0][:, j], ctx[1][:, j])
            h, ckv, kr = fourier_mla_mixer(u, p['mix_w_in'][j], p['q_norm_w'][j], p['w_q_b'][j],
                                           p['kv_norm_w'][j], p['w_kv_b'][j], p['mix_w_out'][j],
                                           ang, layer_ctx)
            if ctx is None:
                ckvs.append(ckv)
                krs.append(kr)
        else:
            if ctx is None:
                h0f = jnp.zeros((b, SSM_HEADS, SSM_HEAD_DIM, SSM_STATE), jnp.float32)
                h0b = h0f
            else:
                h0f, h0b = ctx[2][:, j], ctx[3][:, j]
            h, hf, hb = ssd_mixer(u, p['ssm_w_in'][j], p['ssm_conv_w'][j], p['ssm_conv_b'][j],
                                  p['ssm_dt_bias'][j], p['ssm_a_log'][j], p['ssm_d'][j],
                                  p['ssm_norm_w'][j], p['ssm_w_out'][j], h0f, h0b)
            if ctx is None:
                sfs.append(hf)
                sbs.append(hb)
        x = layer_norm(ALPHA * x + g_m * h, p['ln1_g'][i], p['ln1_b'][i])
        u = x * (1 + sc_f) + sh_f
        if i % 2 == 0:
            f = swiglu(u, p['ffn_w_gate'][j], p['ffn_w_up'][j], p['ffn_w_down'][j])
        else:
            f = moe_swiglu(u, p['moe_router'][j], p['moe_w_gate'][j], p['moe_w_up'][j], p['moe_w_down'][j])
        x = layer_norm(ALPHA * x + g_f * f, p['ln2_g'][i], p['ln2_b'][i])
    return x, ckvs, krs, sfs, sbs


def setup_inputs(seed: int = 0) -> dict:
    key = jax.random.key(seed)
    ks = iter(jax.random.split(key, 48))
    D = D_MODEL

    def nrm(shape, scale=1.0):
        return jax.random.normal(next(ks), shape, jnp.float32) * scale

    def gain(shape):
        return 1.0 + nrm(shape, 0.02)

    x_prompt = nrm((BATCH, SEQ, D))
    x_sample = nrm((DEC_BATCH, DEC_SEQ, D))
    cache_ckv = nrm((DEC_BATCH, N_EVEN, PAST_LEN, KV_RANK))
    cache_krope = nrm((DEC_BATCH, N_EVEN, PAST_LEN, ROPE_DIM))
    state_ssm_fwd = nrm((DEC_BATCH, N_ODD, SSM_HEADS, SSM_HEAD_DIM, SSM_STATE), 0.1)
    state_ssm_bwd = nrm((DEC_BATCH, N_ODD, SSM_HEADS, SSM_HEAD_DIM, SSM_STATE), 0.1)
    c = nrm((DEC_BATCH, D))
    c_ctx = nrm((D,))
    ada_w = nrm((DEPTH, D, 6 * D), 0.5 * D ** -0.5)
    ada_b = nrm((DEPTH, 6 * D), 0.02)
    ln1_g = gain((DEPTH, D))
    ln1_b = nrm((DEPTH, D), 0.02)
    ln2_g = gain((DEPTH, D))
    ln2_b = nrm((DEPTH, D), 0.02)
    mix_w_in = nrm((N_EVEN, D, MIX_IN), D ** -0.5)
    q_norm_w = gain((N_EVEN, Q_RANK))
    w_q_b = nrm((N_EVEN, Q_RANK, MLA_HEADS * QK_DIM), Q_RANK ** -0.5)
    kv_norm_w = gain((N_EVEN, KV_RANK))
    w_kv_b = nrm((N_EVEN, KV_RANK, MLA_HEADS * (NOPE_DIM + V_DIM)), KV_RANK ** -0.5)
    mix_w_out = nrm((N_EVEN, MIX_OUT, D), BETA * MIX_OUT ** -0.5)
    ffn_w_gate = nrm((N_EVEN, D, FFN_DIM), D ** -0.5)
    ffn_w_up = nrm((N_EVEN, D, FFN_DIM), D ** -0.5)
    ffn_w_down = nrm((N_EVEN, FFN_DIM, D), BETA * FFN_DIM ** -0.5)
    ssm_w_in = nrm((N_ODD, D, SSM_IN), D ** -0.5)
    ssm_conv_w = nrm((N_ODD, SSM_CONV, SSM_CONV_DIM), SSM_CONV ** -0.5)
    ssm_conv_b = nrm((N_ODD, SSM_CONV_DIM), 0.02)
    dt = jnp.exp(jax.random.uniform(next(ks), (N_ODD, 2, SSM_HEADS), jnp.float32,
                                    math.log(1e-3), math.log(1e-1)))
    ssm_dt_bias = dt + jnp.log(-jnp.expm1(-dt))
    ssm_a_log = jnp.log(jax.random.uniform(next(ks), (N_ODD, 2, SSM_HEADS), jnp.float32, 1.0, 16.0))
    ssm_d = gain((N_ODD, SSM_HEADS))
    ssm_norm_w = gain((N_ODD, SSM_INNER))
    ssm_w_out = nrm((N_ODD, SSM_INNER, D), BETA * SSM_INNER ** -0.5)
    moe_router = nrm((N_ODD, D, N_EXPERTS), D ** -0.5)
    moe_w_gate = nrm((N_ODD, N_EXPERTS, D, EXPERT_DIM), D ** -0.5)
    moe_w_up = nrm((N_ODD, N_EXPERTS, D, EXPERT_DIM), D ** -0.5)
    moe_w_down = nrm((N_ODD, N_EXPERTS, EXPERT_DIM, D), BETA * EXPERT_DIM ** -0.5)
    return {'x_prompt': x_prompt, 'x_sample': x_sample, 'cache_ckv': cache_ckv,
            'cache_krope': cache_krope, 'state_ssm_fwd': state_ssm_fwd, 'state_ssm_bwd': state_ssm_bwd,
            'c': c, 'c_ctx': c_ctx, 'ada_w': ada_w, 'ada_b': ada_b, 'ln1_g': ln1_g, 'ln1_b': ln1_b,
            'ln2_g': ln2_g, 'ln2_b': ln2_b, 'mix_w_in': mix_w_in, 'q_norm_w': q_norm_w, 'w_q_b': w_q_b,
            'kv_norm_w': kv_norm_w, 'w_kv_b': w_kv_b, 'mix_w_out': mix_w_out, 'ffn_w_gate': ffn_w_gate,
            'ffn_w_up': ffn_w_up, 'ffn_w_down': ffn_w_down, 'ssm_w_in': ssm_w_in,
            'ssm_conv_w': ssm_conv_w, 'ssm_conv_b': ssm_conv_b, 'ssm_dt_bias': ssm_dt_bias,
            'ssm_a_log': ssm_a_log, 'ssm_d': ssm_d, 'ssm_norm_w': ssm_norm_w, 'ssm_w_out': ssm_w_out,
            'moe_router': moe_router, 'moe_w_gate': moe_w_gate, 'moe_w_up': moe_w_up,
            'moe_w_down': moe_w_down}


def reference(x_prompt, x_sample, cache_ckv, cache_krope, state_ssm_fwd, state_ssm_bwd, c, c_ctx,
              ada_w, ada_b, ln1_g, ln1_b, ln2_g, ln2_b, mix_w_in, q_norm_w, w_q_b, kv_norm_w, w_kv_b,
              mix_w_out, ffn_w_gate, ffn_w_up, ffn_w_down, ssm_w_in, ssm_conv_w, ssm_conv_b,
              ssm_dt_bias, ssm_a_log, ssm_d, ssm_norm_w, ssm_w_out, moe_router, moe_w_gate, moe_w_up,
              moe_w_down):
    p = dict(ada_w=ada_w, ada_b=ada_b, ln1_g=ln1_g, ln1_b=ln1_b, ln2_g=ln2_g, ln2_b=ln2_b,
             mix_w_in=mix_w_in, q_norm_w=q_norm_w, w_q_b=w_q_b, kv_norm_w=kv_norm_w, w_kv_b=w_kv_b,
             mix_w_out=mix_w_out, ffn_w_gate=ffn_w_gate, ffn_w_up=ffn_w_up, ffn_w_down=ffn_w_down,
             ssm_w_in=ssm_w_in, ssm_conv_w=ssm_conv_w, ssm_conv_b=ssm_conv_b, ssm_dt_bias=ssm_dt_bias,
             ssm_a_log=ssm_a_log, ssm_d=ssm_d, ssm_norm_w=ssm_norm_w, ssm_w_out=ssm_w_out,
             moe_router=moe_router, moe_w_gate=moe_w_gate, moe_w_up=moe_w_up, moe_w_down=moe_w_down)
    y_prompt, ckvs, krs, sfs, sbs = trunk(x_prompt, c_ctx[None, :], p, None, None)
    new_ckv = jnp.stack(ckvs, axis=1)
    new_krope = jnp.stack(krs, axis=1)
    new_ssm_fwd = jnp.stack(sfs, axis=1)
    new_ssm_bwd = jnp.stack(sbs, axis=1)
    ang = rope_angles(x_sample.shape[1])
    y_sample = trunk(x_sample, c, p, ang, (cache_ckv, cache_krope, state_ssm_fwd, state_ssm_bwd))[0]
    return (y_prompt, y_sample, new_ckv, new_krope, new_ssm_fwd, new_ssm_bwd)
```

```python
import functools

import numpy as np
import jax
import jax.numpy as jnp
from jax import lax
from jax.experimental import pallas as pl
from jax.experimental.pallas import tpu as pltpu

D_MODEL = 2048
BATCH = 16
SEQ = 256
DEPTH = 4
DEC_BATCH = 2
DEC_SEQ = 1024
PAST_LEN = 512
GRID_W = 64
N_EVEN = (DEPTH + 1) // 2
N_ODD = DEPTH // 2
FOURIER_GROUPS = 4
FOURIER_GROUP_DIM = 128
FOURIER_DIM = FOURIER_GROUPS * FOURIER_GROUP_DIM
MLA_HEADS = 12
Q_RANK = 768
KV_RANK = 256
NOPE_DIM = 128
ROPE_DIM = 64
V_DIM = 128
QK_DIM = NOPE_DIM + ROPE_DIM
ATTN_SCALE = QK_DIM ** -0.5
ROPE_THETA = 10000.0
MIX_IN = FOURIER_DIM + Q_RANK + KV_RANK + ROPE_DIM
SSM_INNER = 2 * D_MODEL
SSM_HEAD_DIM = 64
SSM_HEADS = SSM_INNER // SSM_HEAD_DIM
SSM_GROUPS = 8
SSM_STATE = 128
SSM_CONV = 5
SSM_CHUNK = 128
SSM_CONV_DIM = SSM_INNER + 2 * SSM_GROUPS * SSM_STATE
FFN_DIM = 5632
N_EXPERTS = 8
TOP_K = 2
EXPERT_DIM = 4096
ALPHA = (2 * DEPTH) ** 0.25
EPS = 1e-5

F32 = jnp.float32
BF16 = jnp.bfloat16

N_TOK_P = BATCH * SEQ
N_TOK_S = DEC_BATCH * DEC_SEQ
N_TOK = N_TOK_P + N_TOK_S
N_MOD = 6
N_COND = 1 + DEC_BATCH
COND_ROWS = 8
HEADS_PER_GROUP = SSM_HEADS // SSM_GROUPS
GROUP_DIM = HEADS_PER_GROUP * SSM_HEAD_DIM
LANES = 128
V7X_VMEM_LIMIT = 52 * 1024 * 1024
NEG_BIG = -1e30

MOE_TM = 512
MOE_TILES = (N_TOK * TOP_K) // MOE_TM + N_EXPERTS
MOE_ROWS = MOE_TILES * MOE_TM


def _params(sem):
    return pltpu.CompilerParams(dimension_semantics=sem, vmem_limit_bytes=V7X_VMEM_LIMIT)


def _silu(x):
    return x * jax.nn.sigmoid(x)


def _split2(x):
    hi = x.astype(BF16)
    lo = (x - hi.astype(F32)).astype(BF16)
    return hi, lo


def _dot(a, b):
    return jnp.dot(a, b, preferred_element_type=F32)


def _dot_nt(a, b):
    return lax.dot_general(a, b, (((1,), (1,)), ((), ())), preferred_element_type=F32)


def _dot_tn(a, b):
    return lax.dot_general(a, b, (((0,), (0,)), ((), ())), preferred_element_type=F32)


def _dot3(a, b_hi, b_lo):
    a_hi, a_lo = _split2(a)
    return _dot(a_hi, b_hi) + _dot(a_lo, b_hi) + _dot(a_hi, b_lo)


def _cond_of_tile(t, tm):
    p_tiles = N_TOK_P // tm
    per_seq = DEC_SEQ // tm
    return jnp.where(t < p_tiles, 0, 1 + (t - p_tiles) // per_seq)


def _mod_spec(layer, chunk, tm):
    return pl.BlockSpec((None, None, None, 1, D_MODEL),
                        lambda t: (layer, _cond_of_tile(t, tm), chunk, 0, 0))


def _ada_body(c_ref, w_ref, b_ref, o_ref):
    s = _silu(c_ref[...]).astype(BF16)
    o_ref[...] = _dot(s, w_ref[...].astype(BF16)) + b_ref[...]


def _ada(cond, ada_w, ada_b):
    tn = 1024
    n_out = N_MOD * D_MODEL
    return pl.pallas_call(
        _ada_body,
        out_shape=jax.ShapeDtypeStruct((DEPTH, COND_ROWS, n_out), F32),
        grid=(DEPTH, n_out // tn),
        in_specs=[pl.BlockSpec((COND_ROWS, D_MODEL), lambda i, n: (0, 0)),
                  pl.BlockSpec((None, D_MODEL, tn), lambda i, n: (i, 0, n)),
                  pl.BlockSpec((None, 1, tn), lambda i, n: (i, 0, n))],
        out_specs=pl.BlockSpec((None, COND_ROWS, tn), lambda i, n: (i, 0, n)),
        compiler_params=_params(("arbitrary", "arbitrary")),
        name="ada",
    )(cond, ada_w, ada_b.reshape(DEPTH, 1, n_out))


def _mm_body(*refs, n_x, n_w, swiglu):
    x_refs = refs[:n_x]
    w_refs = refs[n_x:n_x + n_w]
    o_ref = refs[n_x + n_w]
    wb_refs = refs[n_x + n_w + 1:]

    @pl.when(pl.program_id(1) == 0)
    def _():
        for w_ref, wb in zip(w_refs, wb_refs):
            wb[...] = w_ref[...].astype(BF16)

    if n_x == 1:
        x = x_refs[0][...]
    else:
        x = jnp.concatenate([r[...] for r in x_refs], axis=1)
    accs = [_dot(x, wb[...]) for wb in wb_refs]
    out = _silu(accs[0]) * accs[1] if swiglu else accs[0]
    o_ref[...] = out.astype(o_ref.dtype)


def _matmul(xs, ws, *, layer, n_cols, tn, tm, out_dtype, swiglu=False, name):
    m = xs[0].shape[0]
    k = sum(x.shape[1] for x in xs)
    assert m % tm == 0 and n_cols % tn == 0 and all(w.shape[1] == k for w in ws)
    in_specs = [pl.BlockSpec((tm, x.shape[1]), lambda n, r: (r, 0)) for x in xs]
    in_specs += [pl.BlockSpec((None, k, tn), lambda n, r: (layer, 0, n)) for _ in ws]
    return pl.pallas_call(
        functools.partial(_mm_body, n_x=len(xs), n_w=len(ws), swiglu=swiglu),
        out_shape=jax.ShapeDtypeStruct((m, n_cols), out_dtype),
        grid=(n_cols // tn, m // tm),
        in_specs=in_specs,
        out_specs=pl.BlockSpec((tm, tn), lambda n, r: (r, n)),
        scratch_shapes=[pltpu.VMEM((k, tn), BF16) for _ in ws],
        compiler_params=_params(("arbitrary", "arbitrary")),
        name=name,
    )(*xs, *ws)


def _modulate_body(x_ref, sc_ref, sh_ref, u_ref):
    u_ref[...] = (x_ref[...] * (1.0 + sc_ref[...]) + sh_ref[...]).astype(BF16)


def _modulate(x, mod5, layer, tm=256):
    row = pl.BlockSpec((tm, D_MODEL), lambda t: (t, 0))
    return pl.pallas_call(
        _modulate_body,
        out_shape=jax.ShapeDtypeStruct((N_TOK, D_MODEL), BF16),
        grid=(N_TOK // tm,),
        in_specs=[row, _mod_spec(layer, 1, tm), _mod_spec(layer, 0, tm)],
        out_specs=row,
        compiler_params=_params(("arbitrary",)),
        name="modulate",
    )(x, mod5, mod5)


def _ln_body(*refs, modulate, route):
    x_ref, h_ref, g_ref, lg_ref, lb_ref = refs[:5]
    refs = refs[5:]
    v = ALPHA * x_ref[...] + g_ref[...] * h_ref[...]
    mu = jnp.mean(v, axis=-1, keepdims=True)
    d = v - mu
    var = jnp.mean(d * d, axis=-1, keepdims=True)
    xn = d * lax.rsqrt(var + EPS) * lg_ref[...] + lb_ref[...]
    if not modulate:
        refs[0][...] = xn
        return
    sc_ref, sh_ref = refs[:2]
    refs = refs[2:]
    u = xn * (1.0 + sc_ref[...]) + sh_ref[...]
    if not route:
        xo_ref, uo_ref = refs
        xo_ref[...] = xn
        uo_ref[...] = u.astype(BF16)
        return
    r_ref, xo_ref, uo_ref, p_ref, i_ref = refs
    xo_ref[...] = xn
    uo_ref[...] = u.astype(BF16)
    r_hi, r_lo = _split2(r_ref[...])
    logits = _dot3(u, r_hi, r_lo)
    lane = lax.broadcasted_iota(jnp.int32, logits.shape, 1)
    l1 = jnp.where(lane < N_EXPERTS, logits, -jnp.inf)
    m1 = jnp.max(l1, axis=-1, keepdims=True)
    i1 = jnp.min(jnp.where(l1 == m1, lane, LANES), axis=-1, keepdims=True)
    l2 = jnp.where(lane == i1, -jnp.inf, l1)
    m2 = jnp.max(l2, axis=-1, keepdims=True)
    i2 = jnp.min(jnp.where(l2 == m2, lane, LANES), axis=-1, keepdims=True)
    e = jnp.exp(m2 - m1)
    p1 = 1.0 / (1.0 + e)
    p2 = e / (1.0 + e)
    p_ref[...] = jnp.where(lane == 0, p1, jnp.where(lane == 1, p2, 0.0))
    i_ref[...] = jnp.where(lane == 0, i1, jnp.where(lane == 1, i2, 0))


def _ln(x, h, mod5, ln_g, ln_b, *, layer, gate_chunk, mod_next=None, router=None, tm=256):
    row = pl.BlockSpec((tm, D_MODEL), lambda t: (t, 0))
    vec = pl.BlockSpec((None, 1, D_MODEL), lambda t: (layer, 0, 0))
    lane_row = pl.BlockSpec((tm, LANES), lambda t: (t, 0))
    in_specs = [row, row, _mod_spec(layer, gate_chunk, tm), vec, vec]
    args = [x, h, mod5, ln_g.reshape(DEPTH, 1, D_MODEL), ln_b.reshape(DEPTH, 1, D_MODEL)]
    out_shape = [jax.ShapeDtypeStruct((N_TOK, D_MODEL), F32)]
    out_specs = [row]
    if mod_next is not None:
        nl, sc_chunk, sh_chunk = mod_next
        in_specs += [_mod_spec(nl, sc_chunk, tm), _mod_spec(nl, sh_chunk, tm)]
        args += [mod5, mod5]
        out_shape.append(jax.ShapeDtypeStruct((N_TOK, D_MODEL), BF16))
        out_specs.append(row)
    if router is not None:
        in_specs.append(pl.BlockSpec((D_MODEL, LANES), lambda t: (0, 0)))
        args.append(router)
        out_shape += [jax.ShapeDtypeStruct((N_TOK, LANES), F32), jax.ShapeDtypeStruct((N_TOK, LANES), jnp.int32)]
        out_specs += [lane_row, lane_row]
    return pl.pallas_call(
        functools.partial(_ln_body, modulate=mod_next is not None, route=router is not None),
        out_shape=out_shape,
        grid=(N_TOK // tm,),
        in_specs=in_specs,
        out_specs=out_specs,
        compiler_params=_params(("arbitrary",)),
        name="ln",
    )(*args)


def _rms(x, w):
    return x * lax.rsqrt(jnp.mean(x * x, axis=-1, keepdims=True) + EPS) * w


def _split_body(p_ref, qw_ref, kw_ref, qn_ref, ckv_ref, kr_ref):
    q0 = FOURIER_DIM
    k0 = q0 + Q_RANK
    r0 = k0 + KV_RANK
    qn_ref[...] = _rms(p_ref[:, q0:k0], qw_ref[...]).astype(BF16)
    ckv_ref[...] = _rms(p_ref[:, k0:r0], kw_ref[...])
    kr_ref[...] = p_ref[:, r0:]


def _split_proj(proj, q_norm_w, kv_norm_w, layer, tm=512):
    return pl.pallas_call(
        _split_body,
        out_shape=[jax.ShapeDtypeStruct((N_TOK, Q_RANK), BF16),
                   jax.ShapeDtypeStruct((N_TOK, KV_RANK), F32),
                   jax.ShapeDtypeStruct((N_TOK, ROPE_DIM), F32)],
        grid=(N_TOK // tm,),
        in_specs=[pl.BlockSpec((tm, MIX_IN), lambda t: (t, 0)),
                  pl.BlockSpec((None, 1, Q_RANK), lambda t: (layer, 0, 0)),
                  pl.BlockSpec((None, 1, KV_RANK), lambda t: (layer, 0, 0))],
        out_specs=[pl.BlockSpec((tm, Q_RANK), lambda t: (t, 0)),
                   pl.BlockSpec((tm, KV_RANK), lambda t: (t, 0)),
                   pl.BlockSpec((tm, ROPE_DIM), lambda t: (t, 0))],
        compiler_params=_params(("arbitrary",)),
        name="split_proj",
    )(proj, q_norm_w.reshape(N_EVEN, 1, Q_RANK), kv_norm_w.reshape(N_EVEN, 1, KV_RANK))


def _bf16_pair(a):
    hi = a.astype(np.float32).astype(jnp.bfloat16)
    lo = (a - hi.astype(np.float64)).astype(np.float32).astype(jnp.bfloat16)
    return np.stack([hi, lo])


@functools.lru_cache(maxsize=None)
def _dft_tables(length):
    k = np.arange(length)
    ang = 2.0 * np.pi * (np.outer(k, k) % length) / length
    scale = 1.0 / np.sqrt(length * FOURIER_GROUP_DIM)
    t_len = np.concatenate([np.cos(ang), -np.sin(ang)], axis=1) * scale
    c = np.arange(FOURIER_GROUP_DIM)
    angc = 2.0 * np.pi * (np.outer(c, c) % FOURIER_GROUP_DIM) / FOURIER_GROUP_DIM
    eye = np.eye(FOURIER_GROUPS)
    t_ch = np.concatenate([np.kron(eye, np.cos(angc)), np.kron(eye, np.sin(angc))], axis=1)
    return _bf16_pair(t_len), _bf16_pair(t_ch)


def _fourier_body(f_ref, tl_ref, tc_ref, o_ref):
    gh = _dot3(f_ref[...], tc_ref[0], tc_ref[1])
    stacked = jnp.concatenate([gh[:, :FOURIER_DIM], gh[:, FOURIER_DIM:]], axis=0)
    s_hi, s_lo = _split2(stacked)
    y = _dot(tl_ref[0], s_hi) + _dot(tl_ref[1], s_hi) + _dot(tl_ref[0], s_lo)
    o_ref[...] = y.astype(BF16)


def _fourier(proj, n_seq, length, row0):
    t_len, t_ch = _dft_tables(length)
    blk0 = row0 // length
    return pl.pallas_call(
        _fourier_body,
        out_shape=jax.ShapeDtypeStruct((n_seq * length, FOURIER_DIM), BF16),
        grid=(n_seq,),
        in_specs=[pl.BlockSpec((length, FOURIER_DIM), lambda b: (blk0 + b, 0)),
                  pl.BlockSpec((2, length, 2 * length), lambda b: (0, 0, 0)),
                  pl.BlockSpec((2, FOURIER_DIM, 2 * FOURIER_DIM), lambda b: (0, 0, 0))],
        out_specs=pl.BlockSpec((length, FOURIER_DIM), lambda b: (b, 0)),
        compiler_params=_params(("arbitrary",)),
        name="fourier",
    )(proj, jnp.asarray(t_len), jnp.asarray(t_ch))


def _rope_pair(x, cos, sin_signed):
    lane = lax.broadcasted_iota(jnp.int32, x.shape, 1)
    first_half = (lane & 31) < 16
    partner = jnp.where(first_half, pltpu.roll(x, LANES - 16, 1), pltpu.roll(x, 16, 1))
    return x * cos + partner * sin_signed


def _attn_body(*refs, rope, ctx):
    qn_ref, qr_ref, kv_ref, kr_ref = refs[:4]
    refs = refs[4:]
    if rope:
        cq_ref, sq_ref, ck_ref, sk_ref = refs[:4]
        refs = refs[4:]
    if ctx:
        kvc_ref, krc_ref = refs[:2]
        refs = refs[2:]
    o_ref = refs[0]

    qr = qr_ref[...]
    kr = kr_ref[...]
    if rope:
        qr = _rope_pair(qr, cq_ref[...], sq_ref[...])
        kr = _rope_pair(jnp.concatenate([kr, kr], axis=1), ck_ref[...], sk_ref[...])[:, :ROPE_DIM]
    qr = qr.astype(BF16)
    kr = kr.astype(BF16)
    if ctx:
        krc = krc_ref[...].astype(BF16)
    for hh in range(2):
        qn = qn_ref[:, hh * NOPE_DIM:(hh + 1) * NOPE_DIM].astype(BF16)
        qrh = qr[:, hh * ROPE_DIM:(hh + 1) * ROPE_DIM]
        c0 = hh * (NOPE_DIM + V_DIM)
        s = (_dot_nt(qn, kv_ref[:, c0:c0 + NOPE_DIM]) + _dot_nt(qrh, kr)) * ATTN_SCALE
        m = jnp.max(s, axis=-1, keepdims=True)
        if ctx:
            sc = (_dot_nt(qn, kvc_ref[:, c0:c0 + NOPE_DIM]) + _dot_nt(qrh, krc)) * ATTN_SCALE
            m = jnp.maximum(m, jnp.max(sc, axis=-1, keepdims=True))
        p = jnp.exp(s - m)
        den = jnp.sum(p, axis=-1, keepdims=True)
        acc = _dot(p.astype(BF16), kv_ref[:, c0 + NOPE_DIM:c0 + NOPE_DIM + V_DIM])
        if ctx:
            pc = jnp.exp(sc - m)
            den = den + jnp.sum(pc, axis=-1, keepdims=True)
            acc = acc + _dot(pc.astype(BF16), kvc_ref[:, c0 + NOPE_DIM:c0 + NOPE_DIM + V_DIM])
        o_ref[:, hh * V_DIM:(hh + 1) * V_DIM] = (acc / den).astype(BF16)


def _attention(q, kv, kr, *, n_seq, length, row0, tq, rope_tabs=None, ctx=None):
    n_qt = length // tq
    qblk0 = row0 // tq
    kblk0 = row0 // length
    pair_w = 2 * NOPE_DIM
    rope_blk0 = MLA_HEADS * NOPE_DIM // LANES
    in_specs = [pl.BlockSpec((tq, pair_w), lambda b, p, i: (qblk0 + b * n_qt + i, p)),
                pl.BlockSpec((tq, LANES), lambda b, p, i: (qblk0 + b * n_qt + i, rope_blk0 + p)),
                pl.BlockSpec((length, 2 * pair_w), lambda b, p, i: (kblk0 + b, p)),
                pl.BlockSpec((length, ROPE_DIM), lambda b, p, i: (kblk0 + b, 0))]
    args = [q, q, kv, kr]
    if rope_tabs is not None:
        cos, sin = rope_tabs
        in_specs += [pl.BlockSpec((tq, LANES), lambda b, p, i: (i, 0)),
                     pl.BlockSpec((tq, LANES), lambda b, p, i: (i, 0)),
                     pl.BlockSpec((length, LANES), lambda b, p, i: (0, 0)),
                     pl.BlockSpec((length, LANES), lambda b, p, i: (0, 0))]
        args += [cos, sin, cos, sin]
    if ctx is not None:
        krc, cblk0 = ctx
        in_specs += [pl.BlockSpec((PAST_LEN, 2 * pair_w), lambda b, p, i: (cblk0 + b, p)),
                     pl.BlockSpec((None, PAST_LEN, ROPE_DIM), lambda b, p, i: (b, 0, 0))]
        args += [kv, krc]
    return pl.pallas_call(
        functools.partial(_attn_body, rope=rope_tabs is not None, ctx=ctx is not None),
        out_shape=jax.ShapeDtypeStruct((n_seq * length, MLA_HEADS * V_DIM), BF16),
        grid=(n_seq, MLA_HEADS // 2, n_qt),
        in_specs=in_specs,
        out_specs=pl.BlockSpec((tq, 2 * V_DIM), lambda b, p, i: (b * n_qt + i, p)),
        compiler_params=_params(("arbitrary", "arbitrary", "arbitrary")),
        name="attention",
    )(*args)


def _rope_tables():
    rows = DEC_SEQ // GRID_W
    row = jnp.repeat(jnp.arange(rows, dtype=F32), GRID_W)
    col = jnp.tile(jnp.arange(GRID_W, dtype=F32), rows)
    inv = ROPE_THETA ** (-jnp.arange(ROPE_DIM // 4, dtype=F32) * 2.0 / (ROPE_DIM // 2))
    ang = jnp.stack([row[:, None] * inv, col[:, None] * inv], axis=1)
    cos, sin = jnp.cos(ang), jnp.sin(ang)
    cos64 = jnp.stack([cos, cos], axis=2).reshape(DEC_SEQ, ROPE_DIM)
    sin64 = jnp.stack([-sin, sin], axis=2).reshape(DEC_SEQ, ROPE_DIM)
    return jnp.tile(cos64, (1, 2)), jnp.tile(sin64, (1, 2))


_CONV_PAD = 8


def _conv_body(x_ref, w_ref, b_ref, o_ref, pad_ref, *, length):
    zeros = jnp.zeros((_CONV_PAD, pad_ref.shape[1]), F32)
    pad_ref[0:_CONV_PAD, :] = zeros
    pad_ref[_CONV_PAD + length:, :] = zeros
    pad_ref[_CONV_PAD:_CONV_PAD + length, :] = x_ref[...]
    acc = b_ref[...] + jnp.zeros(o_ref.shape, F32)
    for k in range(SSM_CONV):
        off = _CONV_PAD + k - SSM_CONV // 2
        acc = acc + w_ref[k:k + 1, :] * pad_ref[off:off + length, :]
    o_ref[...] = _silu(acc)


def _conv(zx, conv_w, conv_b, layer, *, n_seq, length, row0, tc=512):
    blk0 = row0 // length
    c0 = SSM_INNER // tc
    return pl.pallas_call(
        functools.partial(_conv_body, length=length),
        out_shape=jax.ShapeDtypeStruct((n_seq * length, SSM_CONV_DIM), F32),
        grid=(n_seq, SSM_CONV_DIM // tc),
        in_specs=[pl.BlockSpec((length, tc), lambda b, c: (blk0 + b, c0 + c)),
                  pl.BlockSpec((None, SSM_CONV, tc), lambda b, c: (layer, 0, c)),
                  pl.BlockSpec((None, 1, tc), lambda b, c: (layer, 0, c))],
        out_specs=pl.BlockSpec((length, tc), lambda b, c: (b, c)),
        scratch_shapes=[pltpu.VMEM((length + 2 * _CONV_PAD, tc), F32)],
        compiler_params=_params(("arbitrary", "arbitrary")),
        name="conv",
    )(zx, conv_w, conv_b.reshape(N_ODD, 1, SSM_CONV_DIM))


@functools.lru_cache(maxsize=None)
def _head_expand():
    e = np.zeros((2, LANES, GROUP_DIM), np.float32)
    for h in range(HEADS_PER_GROUP):
        e[0, h, h * SSM_HEAD_DIM:(h + 1) * SSM_HEAD_DIM] = 1.0
        e[1, HEADS_PER_GROUP + h, h * SSM_HEAD_DIM:(h + 1) * SSM_HEAD_DIM] = 1.0
    return e.astype(jnp.bfloat16)


def _scan_body(*refs, n_chunks, zero_init, emit_state):
    x_ref, b_ref, c_ref, dt_ref, bias_ref, alog_ref, dskip_ref, e_ref = refs[:8]
    refs = refs[8:]
    if not zero_init:
        h0f_ref, h0b_ref = refs[:2]
        refs = refs[2:]
    y_ref = refs[0]
    refs = refs[1:]
    if emit_state:
        hf_ref, hb_ref = refs[:2]
        refs = refs[2:]
    hf_scr, hb_scr = refs
    q = SSM_CHUNK
    hpg = HEADS_PER_GROUP
    hd = SSM_HEAD_DIM

    rr = lax.broadcasted_iota(jnp.int32, (q, q), 0)
    cc = lax.broadcasted_iota(jnp.int32, (q, q), 1)
    lower = cc <= rr
    upper = cc >= rr
    tri = lower.astype(BF16)
    a_neg = -jnp.exp(alog_ref[...])
    bias = bias_ref[...]
    e_f = e_ref[0]
    e_b = e_ref[1]

    if zero_init:
        hf_scr[...] = jnp.zeros(hf_scr.shape, F32)
        hb_scr[...] = jnp.zeros(hb_scr.shape, F32)
    else:
        hf_scr[...] = h0f_ref[...].reshape(hf_scr.shape)
        hb_scr[...] = h0b_ref[...].reshape(hb_scr.shape)

    def chunk_inputs(c):
        r0 = pl.multiple_of(c * q, q)
        rows = pl.ds(r0, q)
        t = dt_ref[rows, :] + bias
        dt = jnp.maximum(t, 0.0) + jnp.log1p(jnp.exp(-jnp.abs(t)))
        a = dt * a_neg
        a_hi = a.astype(BF16)
        r1 = a - a_hi.astype(F32)
        a_mid = r1.astype(BF16)
        a_lo = (r1 - a_mid.astype(F32)).astype(BF16)
        cs = _dot(tri, a_hi) + _dot(tri, a_mid) + _dot(tri, a_lo)
        return rows, x_ref[rows, :], b_ref[rows, :].astype(BF16), c_ref[rows, :].astype(BF16), dt, a, cs

    def update_state(scr, upd, total, lane0):
        for h in range(hpg):
            blk = slice(h * hd, (h + 1) * hd)
            scr[blk, :] = scr[blk, :] * jnp.exp(total[:, lane0 + h:lane0 + h + 1]) + upd[blk, :]

    def fwd_chunk(c, carry):
        rows, x, bm, cm, dt, a, cs = chunk_inputs(c)
        ecs = cs - a
        cs_t, ecs_t, dt_t = cs.T, ecs.T, dt.T
        cb = _dot_nt(cm, bm)
        lane = lax.broadcasted_iota(jnp.int32, (q, 2 * hd), 1)
        pieces = []
        for pr in range(hpg // 2):
            mats = []
            for h in (2 * pr, 2 * pr + 1):
                lf = jnp.exp(jnp.where(lower, cs[:, h:h + 1] - cs_t[h:h + 1, :], NEG_BIG))
                lb = jnp.exp(jnp.where(upper, ecs_t[hpg + h:hpg + h + 1, :] - ecs[:, hpg + h:hpg + h + 1], NEG_BIG))
                mats.append((cb * (lf * dt_t[h:h + 1, :] + lb * dt_t[hpg + h:hpg + h + 1, :])).astype(BF16))
            xp = x[:, pr * 2 * hd:(pr + 1) * 2 * hd]
            rhs = jnp.concatenate([jnp.where(lane < hd, xp, 0.0), jnp.where(lane >= hd, xp, 0.0)], axis=0)
            pieces.append(_dot(jnp.concatenate(mats, axis=1), rhs.astype(BF16)))
        y = jnp.concatenate(pieces, axis=1)
        y_off = _dot_nt(cm, hf_scr[...].astype(BF16))
        y = y + y_off * _dot(jnp.exp(cs).astype(BF16), e_f) + dskip_ref[...] * x
        y_ref[rows, :] = y
        total = cs[q - 1:q, :]
        w = _dot((jnp.exp(total - cs) * dt).astype(BF16), e_f)
        update_state(hf_scr, _dot_tn((x * w).astype(BF16), bm), total, 0)
        return carry

    def bwd_chunk(i, carry):
        rows, x, bm, cm, dt, a, cs = chunk_inputs(n_chunks - 1 - i)
        ecs = cs - a
        total = cs[q - 1:q, :]
        y_off = _dot_nt(cm, hb_scr[...].astype(BF16))
        y_ref[rows, :] = y_ref[rows, :] + y_off * _dot(jnp.exp(total - ecs).astype(BF16), e_b)
        w = _dot((jnp.exp(ecs) * dt).astype(BF16), e_b)
        update_state(hb_scr, _dot_tn((x * w).astype(BF16), bm), total, hpg)
        return carry

    lax.fori_loop(0, n_chunks, fwd_chunk, 0)
    lax.fori_loop(0, n_chunks, bwd_chunk, 0)
    if emit_state:
        hf_ref[...] = hf_scr[...].reshape(hf_ref.shape)
        hb_ref[...] = hb_scr[...].reshape(hb_ref.shape)


def _scan(xact, dtg, bias_g, alog_g, dskip, *, n_seq, length, dt_row0, h0=None, emit_state):
    dblk0 = dt_row0 // length
    b0 = SSM_INNER // SSM_STATE
    c0 = b0 + SSM_GROUPS
    state_spec = pl.BlockSpec((None, HEADS_PER_GROUP, SSM_HEAD_DIM, SSM_STATE), lambda b, g: (b, g, 0, 0))
    in_specs = [pl.BlockSpec((length, GROUP_DIM), lambda b, g: (b, g)),
                pl.BlockSpec((length, SSM_STATE), lambda b, g: (b, b0 + g)),
                pl.BlockSpec((length, SSM_STATE), lambda b, g: (b, c0 + g)),
                pl.BlockSpec((length, LANES), lambda b, g: (dblk0 + b, g)),
                pl.BlockSpec((1, LANES), lambda b, g: (0, g)),
                pl.BlockSpec((1, LANES), lambda b, g: (0, g)),
                pl.BlockSpec((1, GROUP_DIM), lambda b, g: (0, g)),
                pl.BlockSpec((2, LANES, GROUP_DIM), lambda b, g: (0, 0, 0))]
    args = [xact, xact, xact, dtg, bias_g, alog_g, dskip, jnp.asarray(_head_expand())]
    if h0 is not None:
        in_specs += [state_spec, state_spec]
        args += list(h0)
    out_shape = [jax.ShapeDtypeStruct((n_seq * length, SSM_INNER), F32)]
    out_specs = [pl.BlockSpec((length, GROUP_DIM), lambda b, g: (b, g))]
    if emit_state:
        st = jax.ShapeDtypeStruct((n_seq, SSM_HEADS, SSM_HEAD_DIM, SSM_STATE), F32)
        out_shape += [st, st]
        out_specs += [state_spec, state_spec]
    return pl.pallas_call(
        functools.partial(_scan_body, n_chunks=length // SSM_CHUNK, zero_init=h0 is None, emit_state=emit_state),
        out_shape=out_shape,
        grid=(n_seq, SSM_GROUPS),
        in_specs=in_specs,
        out_specs=out_specs,
        scratch_shapes=[pltpu.VMEM((GROUP_DIM, SSM_STATE), F32), pltpu.VMEM((GROUP_DIM, SSM_STATE), F32)],
        compiler_params=_params(("arbitrary", "arbitrary")),
        name="ssd_scan",
    )(*args)


def _gate_norm_body(y_ref, z_ref, w_ref, o_ref):
    g = y_ref[...] * _silu(z_ref[...])
    o_ref[...] = _rms(g, w_ref[...]).astype(BF16)


def _gate_norm(y, zx, norm_w, layer, tm=256):
    return pl.pallas_call(
        _gate_norm_body,
        out_shape=jax.ShapeDtypeStruct((N_TOK, SSM_INNER), BF16),
        grid=(N_TOK // tm,),
        in_specs=[pl.BlockSpec((tm, SSM_INNER), lambda t: (t, 0)),
                  pl.BlockSpec((tm, SSM_INNER), lambda t: (t, 0)),
                  pl.BlockSpec((None, 1, SSM_INNER), lambda t: (layer, 0, 0))],
        out_specs=pl.BlockSpec((tm, SSM_INNER), lambda t: (t, 0)),
        compiler_params=_params(("arbitrary",)),
        name="gate_norm",
    )(y, zx, norm_w.reshape(N_ODD, 1, SSM_INNER))


def _moe_up_body(e_ref, n_ref, m_ref, first_ref, valid_ref, x_ref, wg_ref, wu_ref, o_ref, gb, ub):
    s = pl.program_id(0)

    @pl.when(first_ref[s] == 1)
    def _():
        gb[...] = wg_ref[...].astype(BF16)
        ub[...] = wu_ref[...].astype(BF16)

    @pl.when(valid_ref[s] == 1)
    def _():
        x = x_ref[...]
        o_ref[...] = (_silu(_dot(x, gb[...])) * _dot(x, ub[...])).astype(BF16)


def _moe_down_body(e_ref, n_ref, m_ref, first_ref, valid_ref, h_ref, g_ref, w_ref, o_ref, wb):
    s = pl.program_id(0)

    @pl.when(first_ref[s] == 1)
    def _():
        wb[...] = w_ref[...].astype(BF16)

    @pl.when(valid_ref[s] == 1)
    def _():
        o_ref[...] = _dot(h_ref[...], wb[...]) * g_ref[...]


def _moe_tables(tiles, n_col_tiles):
    n_steps = MOE_TILES * n_col_tiles
    tile_start = jnp.cumsum(tiles) - tiles
    step_end = jnp.cumsum(tiles * n_col_tiles)
    total = step_end[-1]
    s = jnp.minimum(jnp.arange(n_steps, dtype=jnp.int32), total - 1)
    e = jnp.sum((s[:, None] >= step_end[None, :]).astype(jnp.int32), axis=1)
    local = s - (step_end - tiles * n_col_tiles)[e]
    te = jnp.maximum(tiles[e], 1)
    n = local // te
    k = local - n * te
    m = tile_start[e] + k
    valid = (jnp.arange(n_steps) < total).astype(jnp.int32)
    first = ((k == 0) & (valid == 1)).astype(jnp.int32)
    return e.astype(jnp.int32), n.astype(jnp.int32), m.astype(jnp.int32), first, valid


def _moe(u, probs, idx, w_gate, w_up, w_down, layer):
    tm = MOE_TM
    flat_e = idx.reshape(-1)
    onehot = (flat_e[:, None] == jnp.arange(N_EXPERTS)[None, :]).astype(jnp.int32)
    rank = jnp.sum((jnp.cumsum(onehot, axis=0) - onehot) * onehot, axis=1)
    count = jnp.sum(onehot, axis=0)
    tiles = (count + tm - 1) // tm
    row_start = (jnp.cumsum(tiles) - tiles) * tm
    dest = row_start[flat_e] + rank
    token = jnp.arange(N_TOK * TOP_K, dtype=jnp.int32) // TOP_K
    src = jnp.zeros((MOE_ROWS,), jnp.int32).at[dest].set(token)
    row_gate = jnp.zeros((MOE_ROWS,), F32).at[dest].set(probs.reshape(-1)).reshape(MOE_ROWS, 1)
    x_sorted = jnp.take(u, src, axis=0)

    tn = 512
    n_up = EXPERT_DIM // tn
    tabs = _moe_tables(tiles, n_up)
    hmid = pl.pallas_call(
        _moe_up_body,
        out_shape=jax.ShapeDtypeStruct((MOE_ROWS, EXPERT_DIM), BF16),
        grid_spec=pltpu.PrefetchScalarGridSpec(
            num_scalar_prefetch=5,
            grid=(MOE_TILES * n_up,),
            in_specs=[pl.BlockSpec((tm, D_MODEL), lambda s, e, n, m, f, v: (m[s], 0)),
                      pl.BlockSpec((None, None, D_MODEL, tn), lambda s, e, n, m, f, v: (layer, e[s], 0, n[s])),
                      pl.BlockSpec((None, None, D_MODEL, tn), lambda s, e, n, m, f, v: (layer, e[s], 0, n[s]))],
            out_specs=pl.BlockSpec((tm, tn), lambda s, e, n, m, f, v: (m[s], n[s])),
            scratch_shapes=[pltpu.VMEM((D_MODEL, tn), BF16), pltpu.VMEM((D_MODEL, tn), BF16)]),
        compiler_params=_params(("arbitrary",)),
        name="moe_up",
    )(*tabs, x_sorted, w_gate, w_up)

    n_dn = D_MODEL // tn
    tabs = _moe_tables(tiles, n_dn)
    y_sorted = pl.pallas_call(
        _moe_down_body,
        out_shape=jax.ShapeDtypeStruct((MOE_ROWS, D_MODEL), F32),
        grid_spec=pltpu.PrefetchScalarGridSpec(
            num_scalar_prefetch=5,
            grid=(MOE_TILES * n_dn,),
            in_specs=[pl.BlockSpec((tm, EXPERT_DIM), lambda s, e, n, m, f, v: (m[s], 0)),
                      pl.BlockSpec((tm, 1), lambda s, e, n, m, f, v: (m[s], 0)),
                      pl.BlockSpec((None, None, EXPERT_DIM, tn), lambda s, e, n, m, f, v: (layer, e[s], 0, n[s]))],
            out_specs=pl.BlockSpec((tm, tn), lambda s, e, n, m, f, v: (m[s], n[s])),
            scratch_shapes=[pltpu.VMEM((EXPERT_DIM, tn), BF16)]),
        compiler_params=_params(("arbitrary",)),
        name="moe_down",
    )(*tabs, hmid, row_gate, w_down)

    dest2 = dest.reshape(N_TOK, TOP_K)
    return jnp.take(y_sorted, dest2[:, 0], axis=0) + jnp.take(y_sorted, dest2[:, 1], axis=0)


def _q_perm():
    h = np.arange(MLA_HEADS)[:, None]
    nope = (h * QK_DIM + np.arange(NOPE_DIM)[None, :]).reshape(-1)
    rope = (h * QK_DIM + NOPE_DIM + np.arange(ROPE_DIM)[None, :]).reshape(-1)
    return np.concatenate([nope, rope])


def _even_mixer(u, j, p, cache_ckv, cache_krope, rope_tabs):
    proj = _matmul([u], [p['mix_w_in']], layer=j, n_cols=MIX_IN, tn=MIX_IN, tm=512, out_dtype=F32, name="mix_in")
    qn, ckv, kr = _split_proj(proj, p['q_norm_w'], p['kv_norm_w'], j)
    f = jnp.concatenate([_fourier(proj, BATCH, SEQ, 0), _fourier(proj, DEC_BATCH, DEC_SEQ, N_TOK_P)], axis=0)
    w_q = p['w_q_b'][j][:, _q_perm()][None]
    q = _matmul([qn], [w_q], layer=0, n_cols=MLA_HEADS * QK_DIM, tn=768, tm=1024, out_dtype=F32, name="q_proj")
    ckv_all = jnp.concatenate([ckv, cache_ckv[:, j].reshape(DEC_BATCH * PAST_LEN, KV_RANK)], axis=0).astype(BF16)
    kv = _matmul([ckv_all], [p['w_kv_b']], layer=j, n_cols=MLA_HEADS * (NOPE_DIM + V_DIM), tn=1024, tm=1024,
                 out_dtype=BF16, name="kv_proj")
    o_p = _attention(q, kv, kr, n_seq=BATCH, length=SEQ, row0=0, tq=SEQ)
    o_s = _attention(q, kv, kr, n_seq=DEC_BATCH, length=DEC_SEQ, row0=N_TOK_P, tq=256, rope_tabs=rope_tabs,
                     ctx=(cache_krope[:, j], N_TOK // PAST_LEN))
    o = jnp.concatenate([o_p, o_s], axis=0)
    h = _matmul([f, o], [p['mix_w_out']], layer=j, n_cols=D_MODEL, tn=512, tm=1024, out_dtype=F32, name="mix_out")
    return h, ckv, kr


def _group_dt_columns(w):
    lead = w.shape[:-1]
    w = w.reshape(*lead, 2, SSM_GROUPS, HEADS_PER_GROUP)
    w = jnp.moveaxis(w, -3, -2).reshape(*lead, SSM_GROUPS, 2 * HEADS_PER_GROUP)
    pad = [(0, 0)] * (w.ndim - 1) + [(0, LANES - 2 * HEADS_PER_GROUP)]
    return jnp.pad(w, pad).reshape(*lead, SSM_GROUPS * LANES)


def _odd_mixer(u, j, p, state_f, state_b):
    zx = _matmul([u], [p['ssm_w_in']], layer=j, n_cols=SSM_INNER + SSM_CONV_DIM, tn=512, tm=1024, out_dtype=F32,
                 name="ssm_in")
    w_dt = _group_dt_columns(p['ssm_w_in'][j][:, SSM_INNER + SSM_CONV_DIM:])[None]
    dtg = _matmul([u], [w_dt], layer=0, n_cols=SSM_GROUPS * LANES, tn=512, tm=1024, out_dtype=F32, name="ssm_dt")
    bias_g = _group_dt_columns(p['ssm_dt_bias'][j].reshape(1, 2 * SSM_HEADS))
    alog_g = _group_dt_columns(p['ssm_a_log'][j].reshape(1, 2 * SSM_HEADS))
    dskip = jnp.repeat(p['ssm_d'][j], SSM_HEAD_DIM).reshape(1, SSM_INNER)
    xa_p = _conv(zx, p['ssm_conv_w'], p['ssm_conv_b'], j, n_seq=BATCH, length=SEQ, row0=0)
    xa_s = _conv(zx, p['ssm_conv_w'], p['ssm_conv_b'], j, n_seq=DEC_BATCH, length=DEC_SEQ, row0=N_TOK_P)
    y_p, hf, hb = _scan(xa_p, dtg, bias_g, alog_g, dskip, n_seq=BATCH, length=SEQ, dt_row0=0, emit_state=True)
    (y_s,) = _scan(xa_s, dtg, bias_g, alog_g, dskip, n_seq=DEC_BATCH, length=DEC_SEQ, dt_row0=N_TOK_P,
                   h0=(state_f[:, j], state_b[:, j]), emit_state=False)
    yn = _gate_norm(jnp.concatenate([y_p, y_s], axis=0), zx, p['ssm_norm_w'], j)
    h = _matmul([yn], [p['ssm_w_out']], layer=j, n_cols=D_MODEL, tn=512, tm=1024, out_dtype=F32, name="ssm_out")
    return h, hf, hb


def kernel(x_prompt, x_sample, cache_ckv, cache_krope, state_ssm_fwd, state_ssm_bwd, c, c_ctx, ada_w, ada_b, ln1_g, ln1_b, ln2_g, ln2_b, mix_w_in, q_norm_w, w_q_b, kv_norm_w, w_kv_b, mix_w_out, ffn_w_gate, ffn_w_up, ffn_w_down, ssm_w_in, ssm_conv_w, ssm_conv_b, ssm_dt_bias, ssm_a_log, ssm_d, ssm_norm_w, ssm_w_out, moe_router, moe_w_gate, moe_w_up, moe_w_down):
    p = dict(mix_w_in=mix_w_in, q_norm_w=q_norm_w, w_q_b=w_q_b, kv_norm_w=kv_norm_w, w_kv_b=w_kv_b,
             mix_w_out=mix_w_out, ssm_w_in=ssm_w_in, ssm_conv_w=ssm_conv_w, ssm_conv_b=ssm_conv_b,
             ssm_dt_bias=ssm_dt_bias, ssm_a_log=ssm_a_log, ssm_d=ssm_d, ssm_norm_w=ssm_norm_w, ssm_w_out=ssm_w_out)
    x = jnp.concatenate([x_prompt.reshape(N_TOK_P, D_MODEL), x_sample.reshape(N_TOK_S, D_MODEL)], axis=0)
    cond = jnp.concatenate([c_ctx[None, :], c, jnp.zeros((COND_ROWS - N_COND, D_MODEL), F32)], axis=0)
    mod = _ada(cond, ada_w, ada_b)
    mod5 = mod[:, :N_COND].reshape(DEPTH, N_COND, N_MOD, 1, D_MODEL)
    rope_tabs = _rope_tables()

    u = _modulate(x, mod5, 0)
    ckvs, krs, sfs, sbs = [], [], [], []
    for i in range(DEPTH):
        j = i // 2
        if i % 2 == 0:
            h, ckv, kr = _even_mixer(u, j, p, cache_ckv, cache_krope, rope_tabs)
            ckvs.append(ckv[:N_TOK_P].reshape(BATCH, SEQ, KV_RANK))
            krs.append(kr[:N_TOK_P].reshape(BATCH, SEQ, ROPE_DIM))
            x, u = _ln(x, h, mod5, ln1_g, ln1_b, layer=i, gate_chunk=2, mod_next=(i, 4, 3))
            hmid = _matmul([u], [ffn_w_gate, ffn_w_up], layer=j, n_cols=FFN_DIM, tn=512, tm=1024, out_dtype=BF16,
                           swiglu=True, name="ffn_up")
            f = _matmul([hmid], [ffn_w_down], layer=j, n_cols=D_MODEL, tn=512, tm=512, out_dtype=F32,
                        name="ffn_down")
        else:
            h, hf, hb = _odd_mixer(u, j, p, state_ssm_fwd, state_ssm_bwd)
            sfs.append(hf)
            sbs.append(hb)
            router = jnp.pad(moe_router[j], ((0, 0), (0, LANES - N_EXPERTS)))
            x, u, probs, idx = _ln(x, h, mod5, ln1_g, ln1_b, layer=i, gate_chunk=2, mod_next=(i, 4, 3),
                                   router=router)
            f = _moe(u, probs[:, :TOP_K], idx[:, :TOP_K], moe_w_gate, moe_w_up, moe_w_down, j)
        if i + 1 < DEPTH:
            x, u = _ln(x, f, mod5, ln2_g, ln2_b, layer=i, gate_chunk=5, mod_next=(i + 1, 1, 0))
        else:
            (x,) = _ln(x, f, mod5, ln2_g, ln2_b, layer=i, gate_chunk=5)
    y_prompt = x[:N_TOK_P].reshape(BATCH, SEQ, D_MODEL)
    y_sample = x[N_TOK_P:].reshape(DEC_BATCH, DEC_SEQ, D_MODEL)
    return (y_prompt, y_sample, jnp.stack(ckvs, axis=1), jnp.stack(krs, axis=1),
            jnp.stack(sfs, axis=1), jnp.stack(sbs, axis=1))
```

```python
import functools

import numpy as np
import jax
import jax.numpy as jnp
from jax import lax
from jax.experimental import pallas as pl
from jax.experimental.pallas import tpu as pltpu

D_MODEL = 2048
BATCH = 16
SEQ = 256
DEPTH = 4
DEC_BATCH = 2
DEC_SEQ = 1024
PAST_LEN = 512
GRID_W = 64
N_EVEN = (DEPTH + 1) // 2
N_ODD = DEPTH // 2
FOURIER_GROUPS = 4
FOURIER_GROUP_DIM = 128
FOURIER_DIM = FOURIER_GROUPS * FOURIER_GROUP_DIM
MLA_HEADS = 12
Q_RANK = 768
KV_RANK = 256
NOPE_DIM = 128
ROPE_DIM = 64
V_DIM = 128
QK_DIM = NOPE_DIM + ROPE_DIM
ATTN_SCALE = QK_DIM ** -0.5
ROPE_THETA = 10000.0
MIX_IN = FOURIER_DIM + Q_RANK + KV_RANK + ROPE_DIM
SSM_INNER = 2 * D_MODEL
SSM_HEAD_DIM = 64
SSM_HEADS = SSM_INNER // SSM_HEAD_DIM
SSM_GROUPS = 8
SSM_STATE = 128
SSM_CONV = 5
SSM_CHUNK = 128
SSM_CONV_DIM = SSM_INNER + 2 * SSM_GROUPS * SSM_STATE
FFN_DIM = 5632
N_EXPERTS = 8
TOP_K = 2
EXPERT_DIM = 4096
ALPHA = (2 * DEPTH) ** 0.25
EPS = 1e-5

F32 = jnp.float32
BF16 = jnp.bfloat16

N_TOK_P = BATCH * SEQ
N_TOK_S = DEC_BATCH * DEC_SEQ
N_TOK = N_TOK_P + N_TOK_S
N_MOD = 6
N_COND = 1 + DEC_BATCH
COND_ROWS = 8
HEADS_PER_GROUP = SSM_HEADS // SSM_GROUPS
GROUP_DIM = HEADS_PER_GROUP * SSM_HEAD_DIM
LANES = 128
V7X_VMEM_LIMIT = 52 * 1024 * 1024
NEG_BIG = -1e30

MOE_TM = 512
MOE_TILES = (N_TOK * TOP_K) // MOE_TM + N_EXPERTS
MOE_ROWS = MOE_TILES * MOE_TM

ANY_SPEC = pl.BlockSpec(memory_space=pl.ANY)


def _params(sem):
    return pltpu.CompilerParams(dimension_semantics=sem, vmem_limit_bytes=V7X_VMEM_LIMIT)


def _silu(x):
    return x * jax.nn.sigmoid(x)


def _split2(x):
    hi = x.astype(BF16)
    lo = (x - hi.astype(F32)).astype(BF16)
    return hi, lo


def _dot(a, b):
    return jnp.dot(a, b, preferred_element_type=F32)


def _dot_nt(a, b):
    return lax.dot_general(a, b, (((1,), (1,)), ((), ())), preferred_element_type=F32)


def _dot_tn(a, b):
    return lax.dot_general(a, b, (((0,), (0,)), ((), ())), preferred_element_type=F32)


def _dot3(a, b_hi, b_lo):
    a_hi, a_lo = _split2(a)
    return _dot(a_hi, b_hi) + _dot(a_lo, b_hi) + _dot(a_hi, b_lo)


def _cond_of_tile(t, tm):
    p_tiles = N_TOK_P // tm
    per_seq = DEC_SEQ // tm
    return jnp.where(t < p_tiles, 0, 1 + (t - p_tiles) // per_seq)


def _mod_spec(layer, chunk, tm):
    return pl.BlockSpec((None, None, None, 1, D_MODEL),
                        lambda t: (layer, _cond_of_tile(t, tm), chunk, 0, 0))


def _ada_body(c_ref, w_ref, b_ref, o_ref):
    s = _silu(c_ref[...]).astype(BF16)
    o_ref[...] = _dot(s, w_ref[...].astype(BF16)) + b_ref[...]


def _ada(cond, ada_w, ada_b):
    tn = 1024
    n_out = N_MOD * D_MODEL
    return pl.pallas_call(
        _ada_body,
        out_shape=jax.ShapeDtypeStruct((DEPTH, COND_ROWS, n_out), F32),
        grid=(DEPTH, n_out // tn),
        in_specs=[pl.BlockSpec((COND_ROWS, D_MODEL), lambda i, n: (0, 0)),
                  pl.BlockSpec((None, D_MODEL, tn), lambda i, n: (i, 0, n)),
                  pl.BlockSpec((None, 1, tn), lambda i, n: (i, 0, n))],
        out_specs=pl.BlockSpec((None, COND_ROWS, tn), lambda i, n: (i, 0, n)),
        compiler_params=_params(("arbitrary", "arbitrary")),
        name="ada",
    )(cond, ada_w, ada_b.reshape(DEPTH, 1, n_out))


def _mm_body(*refs, n_x, n_w, swiglu):
    x_refs = refs[:n_x]
    w_refs = refs[n_x:n_x + n_w]
    o_ref = refs[n_x + n_w]
    wb_refs = refs[n_x + n_w + 1:]

    @pl.when(pl.program_id(1) == 0)
    def _():
        for w_ref, wb in zip(w_refs, wb_refs):
            wb[...] = w_ref[...].astype(BF16)

    if n_x == 1:
        x = x_refs[0][...]
    else:
        x = jnp.concatenate([r[...] for r in x_refs], axis=1)
    accs = [_dot(x, wb[...]) for wb in wb_refs]
    out = _silu(accs[0]) * accs[1] if swiglu else accs[0]
    o_ref[...] = out.astype(o_ref.dtype)


def _matmul(xs, ws, *, layer, n_cols, tn, tm, out_dtype, col0=0, swiglu=False, name):
    m = xs[0].shape[0]
    k = sum(x.shape[1] for x in xs)
    assert m % tm == 0 and n_cols % tn == 0 and col0 % tn == 0 and all(w.shape[1] == k for w in ws)
    cb0 = col0 // tn
    in_specs = [pl.BlockSpec((tm, x.shape[1]), lambda n, r: (r, 0)) for x in xs]
    in_specs += [pl.BlockSpec((None, k, tn), lambda n, r: (layer, 0, cb0 + n)) for _ in ws]
    return pl.pallas_call(
        functools.partial(_mm_body, n_x=len(xs), n_w=len(ws), swiglu=swiglu),
        out_shape=jax.ShapeDtypeStruct((m, n_cols), out_dtype),
        grid=(n_cols // tn, m // tm),
        in_specs=in_specs,
        out_specs=pl.BlockSpec((tm, tn), lambda n, r: (r, n)),
        scratch_shapes=[pltpu.VMEM((k, tn), BF16) for _ in ws],
        compiler_params=_params(("arbitrary", "arbitrary")),
        name=name,
    )(*xs, *ws)


_LN_TM = 256
_P_TILES = N_TOK_P // _LN_TM


def _modulate_body(xp_ref, xs_ref, sc_ref, sh_ref, x_ref, u_ref):
    def emit(x):
        x_ref[...] = x
        u_ref[...] = (x * (1.0 + sc_ref[...]) + sh_ref[...]).astype(BF16)

    @pl.when(pl.program_id(0) < _P_TILES)
    def _():
        emit(xp_ref[...])

    @pl.when(pl.program_id(0) >= _P_TILES)
    def _():
        emit(xs_ref[...])


def _modulate(x_prompt, x_sample, mod5, layer):
    tm = _LN_TM
    row = pl.BlockSpec((tm, D_MODEL), lambda t: (t, 0))
    return pl.pallas_call(
        _modulate_body,
        out_shape=[jax.ShapeDtypeStruct((N_TOK, D_MODEL), F32), jax.ShapeDtypeStruct((N_TOK, D_MODEL), BF16)],
        grid=(N_TOK // tm,),
        in_specs=[pl.BlockSpec((tm, D_MODEL), lambda t: (jnp.minimum(t, _P_TILES - 1), 0)),
                  pl.BlockSpec((tm, D_MODEL), lambda t: (jnp.maximum(t - _P_TILES, 0), 0)),
                  _mod_spec(layer, 1, tm), _mod_spec(layer, 0, tm)],
        out_specs=[row, row],
        compiler_params=_params(("arbitrary",)),
        name="modulate",
    )(x_prompt.reshape(N_TOK_P, D_MODEL), x_sample.reshape(N_TOK_S, D_MODEL), mod5, mod5)


def _ln_body(*refs, modulate, route):
    x_ref, h_ref, g_ref, lg_ref, lb_ref = refs[:5]
    refs = refs[5:]
    v = ALPHA * x_ref[...] + g_ref[...] * h_ref[...]
    mu = jnp.mean(v, axis=-1, keepdims=True)
    d = v - mu
    var = jnp.mean(d * d, axis=-1, keepdims=True)
    xn = d * lax.rsqrt(var + EPS) * lg_ref[...] + lb_ref[...]
    if not modulate:
        yp_ref, ys_ref = refs

        @pl.when(pl.program_id(0) < _P_TILES)
        def _():
            yp_ref[...] = xn

        @pl.when(pl.program_id(0) >= _P_TILES)
        def _():
            ys_ref[...] = xn
        return
    sc_ref, sh_ref = refs[:2]
    refs = refs[2:]
    u = xn * (1.0 + sc_ref[...]) + sh_ref[...]
    if not route:
        xo_ref, uo_ref = refs
        xo_ref[...] = xn
        uo_ref[...] = u.astype(BF16)
        return
    r_ref, xo_ref, uo_ref, p_ref, i_ref = refs
    xo_ref[...] = xn
    uo_ref[...] = u.astype(BF16)
    r_hi, r_lo = _split2(r_ref[...])
    logits = _dot3(u, r_hi, r_lo)
    lane = lax.broadcasted_iota(jnp.int32, logits.shape, 1)
    l1 = jnp.where(lane < N_EXPERTS, logits, -jnp.inf)
    m1 = jnp.max(l1, axis=-1, keepdims=True)
    i1 = jnp.min(jnp.where(l1 == m1, lane, LANES), axis=-1, keepdims=True)
    l2 = jnp.where(lane == i1, -jnp.inf, l1)
    m2 = jnp.max(l2, axis=-1, keepdims=True)
    i2 = jnp.min(jnp.where(l2 == m2, lane, LANES), axis=-1, keepdims=True)
    e = jnp.exp(m2 - m1)
    p1 = 1.0 / (1.0 + e)
    p2 = e / (1.0 + e)
    p_ref[...] = jnp.where(lane == 0, p1, jnp.where(lane == 1, p2, 0.0))
    i_ref[...] = jnp.where(lane == 0, i1, jnp.where(lane == 1, i2, 0))


def _ln(x, h, mod5, ln_g, ln_b, *, layer, gate_chunk, mod_next=None, router=None):
    tm = _LN_TM
    row = pl.BlockSpec((tm, D_MODEL), lambda t: (t, 0))
    vec = pl.BlockSpec((None, 1, D_MODEL), lambda t: (layer, 0, 0))
    lane_row = pl.BlockSpec((tm, LANES), lambda t: (t, 0))
    in_specs = [row, row, _mod_spec(layer, gate_chunk, tm), vec, vec]
    args = [x, h, mod5, ln_g.reshape(DEPTH, 1, D_MODEL), ln_b.reshape(DEPTH, 1, D_MODEL)]
    if mod_next is None:
        out_shape = [jax.ShapeDtypeStruct((N_TOK_P, D_MODEL), F32), jax.ShapeDtypeStruct((N_TOK_S, D_MODEL), F32)]
        out_specs = [pl.BlockSpec((tm, D_MODEL), lambda t: (jnp.minimum(t, _P_TILES - 1), 0)),
                     pl.BlockSpec((tm, D_MODEL), lambda t: (jnp.maximum(t - _P_TILES, 0), 0))]
    else:
        nl, sc_chunk, sh_chunk = mod_next
        in_specs += [_mod_spec(nl, sc_chunk, tm), _mod_spec(nl, sh_chunk, tm)]
        args += [mod5, mod5]
        out_shape = [jax.ShapeDtypeStruct((N_TOK, D_MODEL), F32), jax.ShapeDtypeStruct((N_TOK, D_MODEL), BF16)]
        out_specs = [row, row]
    if router is not None:
        in_specs.append(pl.BlockSpec((D_MODEL, LANES), lambda t: (0, 0)))
        args.append(router)
        out_shape += [jax.ShapeDtypeStruct((N_TOK, LANES), F32), jax.ShapeDtypeStruct((N_TOK, LANES), jnp.int32)]
        out_specs += [lane_row, lane_row]
    return pl.pallas_call(
        functools.partial(_ln_body, modulate=mod_next is not None, route=router is not None),
        out_shape=out_shape,
        grid=(N_TOK // tm,),
        in_specs=in_specs,
        out_specs=out_specs,
        compiler_params=_params(("arbitrary",)),
        name="ln",
    )(*args)


def _rms(x, w):
    return x * lax.rsqrt(jnp.mean(x * x, axis=-1, keepdims=True) + EPS) * w


def _split_body(p_ref, qw_ref, kw_ref, qn_ref, ckv_ref, kr_ref):
    q0 = FOURIER_DIM
    k0 = q0 + Q_RANK
    r0 = k0 + KV_RANK
    qn_ref[...] = _rms(p_ref[:, q0:k0], qw_ref[...]).astype(BF16)
    ckv_ref[...] = _rms(p_ref[:, k0:r0], kw_ref[...])
    kr_ref[...] = p_ref[:, r0:]


def _split_proj(proj, q_norm_w, kv_norm_w, layer, tm=512):
    return pl.pallas_call(
        _split_body,
        out_shape=[jax.ShapeDtypeStruct((N_TOK, Q_RANK), BF16),
                   jax.ShapeDtypeStruct((N_TOK, KV_RANK), F32),
                   jax.ShapeDtypeStruct((N_TOK, ROPE_DIM), F32)],
        grid=(N_TOK // tm,),
        in_specs=[pl.BlockSpec((tm, MIX_IN), lambda t: (t, 0)),
                  pl.BlockSpec((None, 1, Q_RANK), lambda t: (layer, 0, 0)),
                  pl.BlockSpec((None, 1, KV_RANK), lambda t: (layer, 0, 0))],
        out_specs=[pl.BlockSpec((tm, Q_RANK), lambda t: (t, 0)),
                   pl.BlockSpec((tm, KV_RANK), lambda t: (t, 0)),
                   pl.BlockSpec((tm, ROPE_DIM), lambda t: (t, 0))],
        compiler_params=_params(("arbitrary",)),
        name="split_proj",
    )(proj, q_norm_w.reshape(N_EVEN, 1, Q_RANK), kv_norm_w.reshape(N_EVEN, 1, KV_RANK))


def _bf16_pair(a):
    hi = a.astype(np.float32).astype(jnp.bfloat16)
    lo = (a - hi.astype(np.float64)).astype(np.float32).astype(jnp.bfloat16)
    return np.stack([hi, lo])


@functools.lru_cache(maxsize=None)
def _dft_tables(length):
    k = np.arange(length)
    ang = 2.0 * np.pi * (np.outer(k, k) % length) / length
    scale = 1.0 / np.sqrt(length * FOURIER_GROUP_DIM)
    t_len = np.concatenate([np.cos(ang), -np.sin(ang)], axis=1) * scale
    c = np.arange(FOURIER_GROUP_DIM)
    angc = 2.0 * np.pi * (np.outer(c, c) % FOURIER_GROUP_DIM) / FOURIER_GROUP_DIM
    eye = np.eye(FOURIER_GROUPS)
    t_ch = np.concatenate([np.kron(eye, np.cos(angc)), np.kron(eye, np.sin(angc))], axis=1)
    return _bf16_pair(t_len), _bf16_pair(t_ch)


def _fourier_body(f_ref, tl_ref, tc_ref, *rest):
    o_ref = rest[-1]
    gh = _dot3(f_ref[...], tc_ref[0], tc_ref[1])
    stacked = jnp.concatenate([gh[:, :FOURIER_DIM], gh[:, FOURIER_DIM:]], axis=0)
    s_hi, s_lo = _split2(stacked)
    y = _dot(tl_ref[0], s_hi) + _dot(tl_ref[1], s_hi) + _dot(tl_ref[0], s_lo)
    o_ref[...] = y.astype(BF16)


def _fourier(proj, n_seq, length, row0, prev=None):
    t_len, t_ch = _dft_tables(length)
    blk0 = row0 // length
    in_specs = [pl.BlockSpec((length, FOURIER_DIM), lambda b: (blk0 + b, 0)),
                pl.BlockSpec((2, length, 2 * length), lambda b: (0, 0, 0)),
                pl.BlockSpec((2, FOURIER_DIM, 2 * FOURIER_DIM), lambda b: (0, 0, 0))]
    args = [proj, jnp.asarray(t_len), jnp.asarray(t_ch)]
    aliases = {}
    if prev is not None:
        in_specs.append(ANY_SPEC)
        args.append(prev)
        aliases = {3: 0}
    return pl.pallas_call(
        _fourier_body,
        out_shape=jax.ShapeDtypeStruct((N_TOK, FOURIER_DIM), BF16),
        grid=(n_seq,),
        in_specs=in_specs,
        out_specs=pl.BlockSpec((length, FOURIER_DIM), lambda b: (blk0 + b, 0)),
        input_output_aliases=aliases,
        compiler_params=_params(("arbitrary",)),
        name="fourier",
    )(*args)


def _rope_pair(x, cos, sin_signed):
    lane = lax.broadcasted_iota(jnp.int32, x.shape, 1)
    first_half = (lane & 31) < 16
    partner = jnp.where(first_half, pltpu.roll(x, LANES - 16, 1), pltpu.roll(x, 16, 1))
    return x * cos + partner * sin_signed


def _attn_body(*refs, rope, ctx):
    qn_ref, qr_ref, kv_ref, kr_ref = refs[:4]
    refs = refs[4:]
    if rope:
        cq_ref, sq_ref, ck_ref, sk_ref = refs[:4]
        refs = refs[4:]
    if ctx:
        kvc_ref, krc_ref = refs[:2]
        refs = refs[2:]
    o_ref = refs[-1]

    qr = qr_ref[...]
    kr = kr_ref[...]
    if rope:
        qr = _rope_pair(qr, cq_ref[...], sq_ref[...])
        kr = _rope_pair(jnp.concatenate([kr, kr], axis=1), ck_ref[...], sk_ref[...])[:, :ROPE_DIM]
    qr = qr.astype(BF16)
    kr = kr.astype(BF16)
    if ctx:
        krc = krc_ref[...].astype(BF16)
    for hh in range(2):
        qn = qn_ref[:, hh * NOPE_DIM:(hh + 1) * NOPE_DIM].astype(BF16)
        qrh = qr[:, hh * ROPE_DIM:(hh + 1) * ROPE_DIM]
        c0 = hh * (NOPE_DIM + V_DIM)
        s = (_dot_nt(qn, kv_ref[:, c0:c0 + NOPE_DIM]) + _dot_nt(qrh, kr)) * ATTN_SCALE
        m = jnp.max(s, axis=-1, keepdims=True)
        if ctx:
            sc = (_dot_nt(qn, kvc_ref[:, c0:c0 + NOPE_DIM]) + _dot_nt(qrh, krc)) * ATTN_SCALE
            m = jnp.maximum(m, jnp.max(sc, axis=-1, keepdims=True))
        p = jnp.exp(s - m)
        den = jnp.sum(p, axis=-1, keepdims=True)
        acc = _dot(p.astype(BF16), kv_ref[:, c0 + NOPE_DIM:c0 + NOPE_DIM + V_DIM])
        if ctx:
            pc = jnp.exp(sc - m)
            den = den + jnp.sum(pc, axis=-1, keepdims=True)
            acc = acc + _dot(pc.astype(BF16), kvc_ref[:, c0 + NOPE_DIM:c0 + NOPE_DIM + V_DIM])
        o_ref[:, hh * V_DIM:(hh + 1) * V_DIM] = (acc / den).astype(BF16)


def _attention(q, kv, kr, *, n_seq, length, row0, tq, rope_tabs=None, ctx=None, prev=None):
    n_qt = length // tq
    qblk0 = row0 // tq
    kblk0 = row0 // length
    pair_w = 2 * NOPE_DIM
    rope_blk0 = MLA_HEADS * NOPE_DIM // LANES
    in_specs = [pl.BlockSpec((tq, pair_w), lambda b, p, i: (qblk0 + b * n_qt + i, p)),
                pl.BlockSpec((tq, LANES), lambda b, p, i: (qblk0 + b * n_qt + i, rope_blk0 + p)),
                pl.BlockSpec((length, 2 * pair_w), lambda b, p, i: (kblk0 + b, p)),
                pl.BlockSpec((length, ROPE_DIM), lambda b, p, i: (kblk0 + b, 0))]
    args = [q, q, kv, kr]
    if rope_tabs is not None:
        cos, sin = rope_tabs
        in_specs += [pl.BlockSpec((tq, LANES), lambda b, p, i: (i, 0)),
                     pl.BlockSpec((tq, LANES), lambda b, p, i: (i, 0)),
                     pl.BlockSpec((length, LANES), lambda b, p, i: (0, 0)),
                     pl.BlockSpec((length, LANES), lambda b, p, i: (0, 0))]
        args += [cos, sin, cos, sin]
    if ctx is not None:
        krc, cblk0 = ctx
        in_specs += [pl.BlockSpec((PAST_LEN, 2 * pair_w), lambda b, p, i: (cblk0 + b, p)),
                     pl.BlockSpec((None, PAST_LEN, ROPE_DIM), lambda b, p, i: (b, 0, 0))]
        args += [kv, krc]
    aliases = {}
    if prev is not None:
        aliases = {len(args): 0}
        in_specs.append(ANY_SPEC)
        args.append(prev)
    return pl.pallas_call(
        functools.partial(_attn_body, rope=rope_tabs is not None, ctx=ctx is not None),
        out_shape=jax.ShapeDtypeStruct((N_TOK, MLA_HEADS * V_DIM), BF16),
        grid=(n_seq, MLA_HEADS // 2, n_qt),
        in_specs=in_specs,
        out_specs=pl.BlockSpec((tq, 2 * V_DIM), lambda b, p, i: (qblk0 + b * n_qt + i, p)),
        input_output_aliases=aliases,
        compiler_params=_params(("arbitrary", "arbitrary", "arbitrary")),
        name="attention",
    )(*args)


def _rope_tables():
    rows = DEC_SEQ // GRID_W
    row = jnp.repeat(jnp.arange(rows, dtype=F32), GRID_W)
    col = jnp.tile(jnp.arange(GRID_W, dtype=F32), rows)
    inv = ROPE_THETA ** (-jnp.arange(ROPE_DIM // 4, dtype=F32) * 2.0 / (ROPE_DIM // 2))
    ang = jnp.stack([row[:, None] * inv, col[:, None] * inv], axis=1)
    cos, sin = jnp.cos(ang), jnp.sin(ang)
    cos64 = jnp.stack([cos, cos], axis=2).reshape(DEC_SEQ, ROPE_DIM)
    sin64 = jnp.stack([-sin, sin], axis=2).reshape(DEC_SEQ, ROPE_DIM)
    return jnp.tile(cos64, (1, 2)), jnp.tile(sin64, (1, 2))


_CONV_PAD = 8


@functools.lru_cache(maxsize=None)
def _head_expand():
    e = np.zeros((2, LANES, GROUP_DIM), np.float32)
    for h in range(HEADS_PER_GROUP):
        e[0, h, h * SSM_HEAD_DIM:(h + 1) * SSM_HEAD_DIM] = 1.0
        e[1, HEADS_PER_GROUP + h, h * SSM_HEAD_DIM:(h + 1) * SSM_HEAD_DIM] = 1.0
    return e.astype(jnp.bfloat16)


def _scan_body(*refs, length, zero_init, emit_state, n_prev):
    (xr_ref, br_ref, cr_ref, wx_ref, wb_ref, wc_ref, bx_ref, bb_ref, bc_ref,
     dt_ref, bias_ref, alog_ref, dskip_ref, e_ref) = refs[:14]
    refs = refs[14:]
    if not zero_init:
        h0f_ref, h0b_ref = refs[:2]
        refs = refs[2:]
    refs = refs[n_prev:]
    y_ref = refs[0]
    refs = refs[1:]
    if emit_state:
        hf_ref, hb_ref = refs[:2]
        refs = refs[2:]
    padx, padb, padc, bm_scr, cm_scr, xwf_scr, xwb_scr, decf_scr, decb_scr = refs
    q = SSM_CHUNK
    n_chunks = length // q
    hpg = HEADS_PER_GROUP
    hd = SSM_HEAD_DIM
    state_shape = (GROUP_DIM, SSM_STATE)

    for raw, pad in ((xr_ref, padx), (br_ref, padb), (cr_ref, padc)):
        zeros = jnp.zeros((_CONV_PAD, pad.shape[1]), F32)
        pad[0:_CONV_PAD, :] = zeros
        pad[_CONV_PAD + length:, :] = zeros
        pad[_CONV_PAD:_CONV_PAD + length, :] = raw[...]

    rr = lax.broadcasted_iota(jnp.int32, (q, q), 0)
    cc = lax.broadcasted_iota(jnp.int32, (q, q), 1)
    lower = cc <= rr
    upper = cc >= rr
    tri = lower.astype(BF16)
    pair_lane = lax.broadcasted_iota(jnp.int32, (q, 2 * hd), 1)
    a_neg = -jnp.exp(alog_ref[...])
    bias = bias_ref[...]
    e_f = e_ref[0]
    e_b = e_ref[1]

    def conv_silu(pad, w_ref, b_ref, r0):
        acc = b_ref[...]
        for k in range(SSM_CONV):
            off = r0 + _CONV_PAD + k - SSM_CONV // 2
            acc = acc + w_ref[k:k + 1, :] * pad[off:off + q, :]
        return _silu(acc)

    totals = []
    for c in range(n_chunks):
        r0 = c * q
        rows = slice(r0, r0 + q)
        x = conv_silu(padx, wx_ref, bx_ref, r0)
        bm = conv_silu(padb, wb_ref, bb_ref, r0).astype(BF16)
        cm = conv_silu(padc, wc_ref, bc_ref, r0).astype(BF16)
        bm_scr[rows, :] = bm
        cm_scr[rows, :] = cm
        t = dt_ref[rows, :] + bias
        dt = jnp.maximum(t, 0.0) + jnp.log1p(jnp.exp(-jnp.abs(t)))
        a = dt * a_neg
        a_hi = a.astype(BF16)
        r1 = a - a_hi.astype(F32)
        a_mid = r1.astype(BF16)
        a_lo = (r1 - a_mid.astype(F32)).astype(BF16)
        cs = _dot(tri, a_hi) + _dot(tri, a_mid) + _dot(tri, a_lo)
        ecs = cs - a
        total = cs[q - 1:q, :]
        totals.append(jnp.exp(total))
        cs_t, ecs_t, dt_t = cs.T, ecs.T, dt.T
        cb = _dot_nt(cm, bm)
        pieces = []
        for pr in range(hpg // 2):
            mats = []
            for h in (2 * pr, 2 * pr + 1):
                lf = jnp.exp(jnp.where(lower, cs[:, h:h + 1] - cs_t[h:h + 1, :], NEG_BIG))
                lb = jnp.exp(jnp.where(upper, ecs_t[hpg + h:hpg + h + 1, :] - ecs[:, hpg + h:hpg + h + 1], NEG_BIG))
                mats.append((cb * (lf * dt_t[h:h + 1, :] + lb * dt_t[hpg + h:hpg + h + 1, :])).astype(BF16))
            xp = x[:, pr * 2 * hd:(pr + 1) * 2 * hd]
            rhs = jnp.concatenate([jnp.where(pair_lane < hd, xp, 0.0), jnp.where(pair_lane >= hd, xp, 0.0)], axis=0)
            pieces.append(_dot(jnp.concatenate(mats, axis=1), rhs.astype(BF16)))
        y_ref[rows, :] = jnp.concatenate(pieces, axis=1) + dskip_ref[...] * x
        decf_scr[rows, :] = _dot(jnp.exp(cs).astype(BF16), e_f).astype(BF16)
        decb_scr[rows, :] = _dot(jnp.exp(total - ecs).astype(BF16), e_b).astype(BF16)
        xwf_scr[rows, :] = (x * _dot((jnp.exp(total - cs) * dt).astype(BF16), e_f)).astype(BF16)
        xwb_scr[rows, :] = (x * _dot((jnp.exp(ecs) * dt).astype(BF16), e_b)).astype(BF16)

    def step(state, c, dec_scr, xw_scr, lane0):
        rows = slice(c * q, (c + 1) * q)
        y_off = _dot_nt(cm_scr[rows, :], state.astype(BF16))
        y_ref[rows, :] = y_ref[rows, :] + y_off * dec_scr[rows, :].astype(F32)
        upd = _dot_tn(xw_scr[rows, :], bm_scr[rows, :])
        decay = totals[c]
        return jnp.concatenate(
            [state[h * hd:(h + 1) * hd, :] * decay[:, lane0 + h:lane0 + h + 1] + upd[h * hd:(h + 1) * hd, :]
             for h in range(hpg)], axis=0)

    if zero_init:
        hf = jnp.zeros(state_shape, F32)
        hb = jnp.zeros(state_shape, F32)
    else:
        hf = h0f_ref[...].reshape(state_shape)
        hb = h0b_ref[...].reshape(state_shape)
    for i in range(n_chunks):
        hf = step(hf, i, decf_scr, xwf_scr, 0)
        hb = step(hb, n_chunks - 1 - i, decb_scr, xwb_scr, hpg)
    if emit_state:
        hf_ref[...] = hf.reshape(hf_ref.shape)
        hb_ref[...] = hb.reshape(hb_ref.shape)


def _scan(xbc, dtg, conv_w, conv_b, bias_g, alog_g, dskip, layer, *, n_seq, length, row0, h0=None,
          y_prev=None, state_prev=None, emit_state):
    blk0 = row0 // length
    b0 = SSM_INNER // SSM_STATE
    c0 = b0 + SSM_GROUPS
    seq = lambda width, col: pl.BlockSpec((length, width), lambda b, g: (blk0 + b, col + g))
    cw = lambda width, col: pl.BlockSpec((None, SSM_CONV, width), lambda b, g: (layer, 0, col + g))
    cbias = lambda width, col: pl.BlockSpec((None, 1, width), lambda b, g: (layer, 0, col + g))
    grp = lambda width: pl.BlockSpec((None, 1, width), lambda b, g: (layer, 0, g))
    h0_spec = pl.BlockSpec((None, None, HEADS_PER_GROUP, SSM_HEAD_DIM, SSM_STATE), lambda b, g: (b, layer, g, 0, 0))
    in_specs = [seq(GROUP_DIM, 0), seq(SSM_STATE, b0), seq(SSM_STATE, c0),
                cw(GROUP_DIM, 0), cw(SSM_STATE, b0), cw(SSM_STATE, c0),
                cbias(GROUP_DIM, 0), cbias(SSM_STATE, b0), cbias(SSM_STATE, c0),
                seq(LANES, 0), grp(LANES), grp(LANES), grp(GROUP_DIM),
                pl.BlockSpec((2, LANES, GROUP_DIM), lambda b, g: (0, 0, 0))]
    cb3 = conv_b.reshape(N_ODD, 1, SSM_CONV_DIM)
    args = [xbc, xbc, xbc, conv_w, conv_w, conv_w, cb3, cb3, cb3, dtg, bias_g, alog_g, dskip,
            jnp.asarray(_head_expand())]
    if h0 is not None:
        in_specs += [h0_spec, h0_spec]
        args += list(h0)
    out_shape = [jax.ShapeDtypeStruct((N_TOK, SSM_INNER), F32)]
    out_specs = [seq(GROUP_DIM, 0)]
    if emit_state:
        st = jax.ShapeDtypeStruct((n_seq, N_ODD, SSM_HEADS, SSM_HEAD_DIM, SSM_STATE), F32)
        out_shape += [st, st]
        out_specs += [h0_spec, h0_spec]
    aliases = {}
    prevs = ([] if y_prev is None else [y_prev]) + ([] if state_prev is None else list(state_prev))
    for k, prev in enumerate(prevs):
        aliases[len(args)] = k if y_prev is not None else k + 1
        in_specs.append(ANY_SPEC)
        args.append(prev)
    pad_rows = length + 2 * _CONV_PAD
    scratch = [pltpu.VMEM((pad_rows, GROUP_DIM), F32), pltpu.VMEM((pad_rows, SSM_STATE), F32),
               pltpu.VMEM((pad_rows, SSM_STATE), F32),
               pltpu.VMEM((length, SSM_STATE), BF16), pltpu.VMEM((length, SSM_STATE), BF16),
               pltpu.VMEM((length, GROUP_DIM), BF16), pltpu.VMEM((length, GROUP_DIM), BF16),
               pltpu.VMEM((length, GROUP_DIM), BF16), pltpu.VMEM((length, GROUP_DIM), BF16)]
    return pl.pallas_call(
        functools.partial(_scan_body, length=length, zero_init=h0 is None, emit_state=emit_state,
                          n_prev=len(prevs)),
        out_shape=out_shape,
        grid=(n_seq, SSM_GROUPS),
        in_specs=in_specs,
        out_specs=out_specs,
        scratch_shapes=scratch,
        input_output_aliases=aliases,
        compiler_params=_params(("arbitrary", "arbitrary")),
        name="ssd_scan",
    )(*args)


def _gate_norm_body(y_ref, z_ref, w_ref, o_ref):
    g = y_ref[...] * _silu(z_ref[...].astype(F32))
    o_ref[...] = _rms(g, w_ref[...]).astype(BF16)


def _gate_norm(y, z, norm_w, layer, tm=256):
    return pl.pallas_call(
        _gate_norm_body,
        out_shape=jax.ShapeDtypeStruct((N_TOK, SSM_INNER), BF16),
        grid=(N_TOK // tm,),
        in_specs=[pl.BlockSpec((tm, SSM_INNER), lambda t: (t, 0)),
                  pl.BlockSpec((tm, SSM_INNER), lambda t: (t, 0)),
                  pl.BlockSpec((None, 1, SSM_INNER), lambda t: (layer, 0, 0))],
        out_specs=pl.BlockSpec((tm, SSM_INNER), lambda t: (t, 0)),
        compiler_params=_params(("arbitrary",)),
        name="gate_norm",
    )(y, z, norm_w.reshape(N_ODD, 1, SSM_INNER))


def _moe_up_body(e_ref, n_ref, m_ref, first_ref, valid_ref, x_ref, wg_ref, wu_ref, o_ref, gb, ub):
    s = pl.program_id(0)

    @pl.when(first_ref[s] == 1)
    def _():
        gb[...] = wg_ref[...].astype(BF16)
        ub[...] = wu_ref[...].astype(BF16)

    @pl.when(valid_ref[s] == 1)
    def _():
        x = x_ref[...]
        o_ref[...] = (_silu(_dot(x, gb[...])) * _dot(x, ub[...])).astype(BF16)


def _moe_down_body(e_ref, n_ref, m_ref, first_ref, valid_ref, h_ref, g_ref, w_ref, o_ref, wb):
    s = pl.program_id(0)

    @pl.when(first_ref[s] == 1)
    def _():
        wb[...] = w_ref[...].astype(BF16)

    @pl.when(valid_ref[s] == 1)
    def _():
        o_ref[...] = _dot(h_ref[...], wb[...]) * g_ref[...]


def _moe_tables(tiles, n_col_tiles):
    n_steps = MOE_TILES * n_col_tiles
    tile_start = jnp.cumsum(tiles) - tiles
    step_end = jnp.cumsum(tiles * n_col_tiles)
    total = step_end[-1]
    s = jnp.minimum(jnp.arange(n_steps, dtype=jnp.int32), total - 1)
    e = jnp.sum((s[:, None] >= step_end[None, :]).astype(jnp.int32), axis=1)
    local = s - (step_end - tiles * n_col_tiles)[e]
    te = jnp.maximum(tiles[e], 1)
    n = local // te
    k = local - n * te
    m = tile_start[e] + k
    valid = (jnp.arange(n_steps) < total).astype(jnp.int32)
    first = ((k == 0) & (valid == 1)).astype(jnp.int32)
    return e.astype(jnp.int32), n.astype(jnp.int32), m.astype(jnp.int32), first, valid


def _moe(u, probs, idx, w_gate, w_up, w_down, layer):
    tm = MOE_TM
    flat_e = idx.reshape(-1)
    onehot = (flat_e[:, None] == jnp.arange(N_EXPERTS)[None, :]).astype(jnp.int32)
    rank = jnp.sum((jnp.cumsum(onehot, axis=0) - onehot) * onehot, axis=1)
    count = jnp.sum(onehot, axis=0)
    tiles = (count + tm - 1) // tm
    row_start = (jnp.cumsum(tiles) - tiles) * tm
    dest = row_start[flat_e] + rank
    token = jnp.arange(N_TOK * TOP_K, dtype=jnp.int32) // TOP_K
    src = jnp.zeros((MOE_ROWS,), jnp.int32).at[dest].set(token, mode="promise_in_bounds", unique_indices=True)
    row_gate = jnp.zeros((MOE_ROWS,), F32).at[dest].set(probs.reshape(-1), mode="promise_in_bounds",
                                                        unique_indices=True).reshape(MOE_ROWS, 1)
    x_sorted = u.at[src].get(mode="promise_in_bounds")

    tn = 1024
    n_up = EXPERT_DIM // tn
    tabs = _moe_tables(tiles, n_up)
    w_spec = pl.BlockSpec((None, None, D_MODEL, tn), lambda s, e, n, m, f, v: (layer, e[s], 0, n[s]))
    hmid = pl.pallas_call(
        _moe_up_body,
        out_shape=jax.ShapeDtypeStruct((MOE_ROWS, EXPERT_DIM), BF16),
        grid_spec=pltpu.PrefetchScalarGridSpec(
            num_scalar_prefetch=5,
            grid=(MOE_TILES * n_up,),
            in_specs=[pl.BlockSpec((tm, D_MODEL), lambda s, e, n, m, f, v: (m[s], 0)), w_spec, w_spec],
            out_specs=pl.BlockSpec((tm, tn), lambda s, e, n, m, f, v: (m[s], n[s])),
            scratch_shapes=[pltpu.VMEM((D_MODEL, tn), BF16), pltpu.VMEM((D_MODEL, tn), BF16)]),
        compiler_params=_params(("arbitrary",)),
        name="moe_up",
    )(*tabs, x_sorted, w_gate, w_up)

    tn = 512
    n_dn = D_MODEL // tn
    tabs = _moe_tables(tiles, n_dn)
    y_sorted = pl.pallas_call(
        _moe_down_body,
        out_shape=jax.ShapeDtypeStruct((MOE_ROWS, D_MODEL), F32),
        grid_spec=pltpu.PrefetchScalarGridSpec(
            num_scalar_prefetch=5,
            grid=(MOE_TILES * n_dn,),
            in_specs=[pl.BlockSpec((tm, EXPERT_DIM), lambda s, e, n, m, f, v: (m[s], 0)),
                      pl.BlockSpec((tm, 1), lambda s, e, n, m, f, v: (m[s], 0)),
                      pl.BlockSpec((None, None, EXPERT_DIM, tn), lambda s, e, n, m, f, v: (layer, e[s], 0, n[s]))],
            out_specs=pl.BlockSpec((tm, tn), lambda s, e, n, m, f, v: (m[s], n[s])),
            scratch_shapes=[pltpu.VMEM((EXPERT_DIM, tn), BF16)]),
        compiler_params=_params(("arbitrary",)),
        name="moe_down",
    )(*tabs, hmid, row_gate, w_down)

    dest2 = dest.reshape(N_TOK, TOP_K)
    return (y_sorted.at[dest2[:, 0]].get(mode="promise_in_bounds")
            + y_sorted.at[dest2[:, 1]].get(mode="promise_in_bounds"))


def _q_perm():
    h = np.arange(MLA_HEADS)[:, None]
    nope = (h * QK_DIM + np.arange(NOPE_DIM)[None, :]).reshape(-1)
    rope = (h * QK_DIM + NOPE_DIM + np.arange(ROPE_DIM)[None, :]).reshape(-1)
    return np.concatenate([nope, rope])


def _even_mixer(u, j, p, cache_ckv, cache_krope, rope_tabs):
    proj = _matmul([u], [p['mix_w_in']], layer=j, n_cols=MIX_IN, tn=MIX_IN, tm=512, out_dtype=F32, name="mix_in")
    qn, ckv, kr = _split_proj(proj, p['q_norm_w'], p['kv_norm_w'], j)
    f = _fourier(proj, BATCH, SEQ, 0)
    f = _fourier(proj, DEC_BATCH, DEC_SEQ, N_TOK_P, prev=f)
    q = _matmul([qn], [p['w_q_perm']], layer=j, n_cols=MLA_HEADS * QK_DIM, tn=768, tm=1024, out_dtype=F32,
                name="q_proj")
    ckv_all = jnp.concatenate([ckv, cache_ckv[:, j].reshape(DEC_BATCH * PAST_LEN, KV_RANK)], axis=0).astype(BF16)
    kv = _matmul([ckv_all], [p['w_kv_b']], layer=j, n_cols=MLA_HEADS * (NOPE_DIM + V_DIM), tn=1024, tm=1024,
                 out_dtype=BF16, name="kv_proj")
    o = _attention(q, kv, kr, n_seq=BATCH, length=SEQ, row0=0, tq=SEQ)
    o = _attention(q, kv, kr, n_seq=DEC_BATCH, length=DEC_SEQ, row0=N_TOK_P, tq=256, rope_tabs=rope_tabs,
                   ctx=(cache_krope[:, j], N_TOK // PAST_LEN), prev=o)
    h = _matmul([f, o], [p['mix_w_out']], layer=j, n_cols=D_MODEL, tn=512, tm=1024, out_dtype=F32, name="mix_out")
    return h, ckv, kr


def _group_dt_columns(w):
    lead = w.shape[:-1]
    w = w.reshape(*lead, 2, SSM_GROUPS, HEADS_PER_GROUP)
    w = jnp.moveaxis(w, -3, -2).reshape(*lead, SSM_GROUPS, 2 * HEADS_PER_GROUP)
    pad = [(0, 0)] * (w.ndim - 1) + [(0, LANES - 2 * HEADS_PER_GROUP)]
    return jnp.pad(w, pad).reshape(*lead, SSM_GROUPS * LANES)


def _odd_mixer(u, j, p, state_f, state_b, state_prev):
    z = _matmul([u], [p['ssm_w_in']], layer=j, n_cols=SSM_INNER, tn=512, tm=1024, out_dtype=BF16, name="ssm_in_z")
    xbc = _matmul([u], [p['ssm_w_in']], layer=j, n_cols=SSM_CONV_DIM, col0=SSM_INNER, tn=512, tm=1024,
                  out_dtype=F32, name="ssm_in_xbc")
    dtg = _matmul([u], [p['ssm_w_dtg']], layer=j, n_cols=SSM_GROUPS * LANES, tn=512, tm=1024, out_dtype=F32,
                  name="ssm_dt")
    common = (xbc, dtg, p['ssm_conv_w'], p['ssm_conv_b'], p['ssm_bias_g'], p['ssm_alog_g'], p['ssm_dskip'], j)
    y, hf, hb = _scan(*common, n_seq=BATCH, length=SEQ, row0=0, state_prev=state_prev, emit_state=True)
    (y,) = _scan(*common, n_seq=DEC_BATCH, length=DEC_SEQ, row0=N_TOK_P, h0=(state_f, state_b), y_prev=y,
                 emit_state=False)
    yn = _gate_norm(y, z, p['ssm_norm_w'], j)
    h = _matmul([yn], [p['ssm_w_out']], layer=j, n_cols=D_MODEL, tn=512, tm=1024, out_dtype=F32, name="ssm_out")
    return h, hf, hb


def kernel(x_prompt, x_sample, cache_ckv, cache_krope, state_ssm_fwd, state_ssm_bwd, c, c_ctx, ada_w, ada_b, ln1_g, ln1_b, ln2_g, ln2_b, mix_w_in, q_norm_w, w_q_b, kv_norm_w, w_kv_b, mix_w_out, ffn_w_gate, ffn_w_up, ffn_w_down, ssm_w_in, ssm_conv_w, ssm_conv_b, ssm_dt_bias, ssm_a_log, ssm_d, ssm_norm_w, ssm_w_out, moe_router, moe_w_gate, moe_w_up, moe_w_down):
    p = dict(mix_w_in=mix_w_in, q_norm_w=q_norm_w, w_q_perm=w_q_b[:, :, _q_perm()], kv_norm_w=kv_norm_w,
             w_kv_b=w_kv_b, mix_w_out=mix_w_out, ssm_w_in=ssm_w_in, ssm_conv_w=ssm_conv_w, ssm_conv_b=ssm_conv_b,
             ssm_w_dtg=_group_dt_columns(ssm_w_in[:, :, SSM_INNER + SSM_CONV_DIM:]),
             ssm_bias_g=_group_dt_columns(ssm_dt_bias.reshape(N_ODD, 1, 2 * SSM_HEADS)),
             ssm_alog_g=_group_dt_columns(ssm_a_log.reshape(N_ODD, 1, 2 * SSM_HEADS)),
             ssm_dskip=jnp.repeat(ssm_d, SSM_HEAD_DIM, axis=1).reshape(N_ODD, 1, SSM_INNER),
             ssm_norm_w=ssm_norm_w, ssm_w_out=ssm_w_out)
    cond = jnp.concatenate([c_ctx[None, :], c, jnp.zeros((COND_ROWS - N_COND, D_MODEL), F32)], axis=0)
    mod = _ada(cond, ada_w, ada_b)
    mod5 = mod[:, :N_COND].reshape(DEPTH, N_COND, N_MOD, 1, D_MODEL)
    rope_tabs = _rope_tables()

    x, u = _modulate(x_prompt, x_sample, mod5, 0)
    ckvs, krs = [], []
    states = None
    for i in range(DEPTH):
        j = i // 2
        if i % 2 == 0:
            h, ckv, kr = _even_mixer(u, j, p, cache_ckv, cache_krope, rope_tabs)
            ckvs.append(ckv[:N_TOK_P].reshape(BATCH, SEQ, KV_RANK))
            krs.append(kr[:N_TOK_P].reshape(BATCH, SEQ, ROPE_DIM))
            x, u = _ln(x, h, mod5, ln1_g, ln1_b, layer=i, gate_chunk=2, mod_next=(i, 4, 3))
            hmid = _matmul([u], [ffn_w_gate, ffn_w_up], layer=j, n_cols=FFN_DIM, tn=512, tm=1024, out_dtype=BF16,
                           swiglu=True, name="ffn_up")
            f = _matmul([hmid], [ffn_w_down], layer=j, n_cols=D_MODEL, tn=512, tm=512, out_dtype=F32,
                        name="ffn_down")
        else:
            h, hf, hb = _odd_mixer(u, j, p, state_ssm_fwd, state_ssm_bwd, states)
            states = (hf, hb)
            router = jnp.pad(moe_router[j], ((0, 0), (0, LANES - N_EXPERTS)))
            x, u, probs, idx = _ln(x, h, mod5, ln1_g, ln1_b, layer=i, gate_chunk=2, mod_next=(i, 4, 3),
                                   router=router)
            f = _moe(u, probs[:, :TOP_K], idx[:, :TOP_K], moe_w_gate, moe_w_up, moe_w_down, j)
        if i + 1 < DEPTH:
            x, u = _ln(x, f, mod5, ln2_g, ln2_b, layer=i, gate_chunk=5, mod_next=(i + 1, 1, 0))
        else:
            y_prompt, y_sample = _ln(x, f, mod5, ln2_g, ln2_b, layer=i, gate_chunk=5)
    return (y_prompt.reshape(BATCH, SEQ, D_MODEL), y_sample.reshape(DEC_BATCH, DEC_SEQ, D_MODEL),
            jnp.stack(ckvs, axis=1), jnp.stack(krs, axis=1), states[0], states[1])
```

```python
import functools

import numpy as np
import jax
import jax.numpy as jnp
from jax import lax
from jax.experimental import pallas as pl
from jax.experimental.pallas import tpu as pltpu

D_MODEL = 2048
BATCH = 16
SEQ = 256
DEPTH = 4
DEC_BATCH = 2
DEC_SEQ = 1024
PAST_LEN = 512
GRID_W = 64
N_EVEN = (DEPTH + 1) // 2
N_ODD = DEPTH // 2
FOURIER_GROUPS = 4
FOURIER_GROUP_DIM = 128
FOURIER_DIM = FOURIER_GROUPS * FOURIER_GROUP_DIM
MLA_HEADS = 12
Q_RANK = 768
KV_RANK = 256
NOPE_DIM = 128
ROPE_DIM = 64
V_DIM = 128
QK_DIM = NOPE_DIM + ROPE_DIM
ATTN_SCALE = QK_DIM ** -0.5
ROPE_THETA = 10000.0
MIX_IN = FOURIER_DIM + Q_RANK + KV_RANK + ROPE_DIM
SSM_INNER = 2 * D_MODEL
SSM_HEAD_DIM = 64
SSM_HEADS = SSM_INNER // SSM_HEAD_DIM
SSM_GROUPS = 8
SSM_STATE = 128
SSM_CONV = 5
SSM_CHUNK = 128
SSM_CONV_DIM = SSM_INNER + 2 * SSM_GROUPS * SSM_STATE
FFN_DIM = 5632
N_EXPERTS = 8
TOP_K = 2
EXPERT_DIM = 4096
ALPHA = (2 * DEPTH) ** 0.25
EPS = 1e-5

F32 = jnp.float32
BF16 = jnp.bfloat16

N_TOK_P = BATCH * SEQ
N_TOK_S = DEC_BATCH * DEC_SEQ
N_TOK = N_TOK_P + N_TOK_S
N_MOD = 6
N_COND = 1 + DEC_BATCH
COND_ROWS = 8
HEADS_PER_GROUP = SSM_HEADS // SSM_GROUPS
GROUP_DIM = HEADS_PER_GROUP * SSM_HEAD_DIM
LANES = 128
V7X_VMEM_LIMIT = 52 * 1024 * 1024
NEG_BIG = -1e30

MOE_TM = 512
MOE_TILES = (N_TOK * TOP_K) // MOE_TM + N_EXPERTS
MOE_ROWS = MOE_TILES * MOE_TM

ANY_SPEC = pl.BlockSpec(memory_space=pl.ANY)


def _params(sem):
    return pltpu.CompilerParams(dimension_semantics=sem, vmem_limit_bytes=V7X_VMEM_LIMIT)


def _silu(x):
    return x * jax.nn.sigmoid(x)


def _split2(x):
    hi = x.astype(BF16)
    lo = (x - hi.astype(F32)).astype(BF16)
    return hi, lo


def _dot(a, b):
    return jnp.dot(a, b, preferred_element_type=F32)


def _dot_nt(a, b):
    return lax.dot_general(a, b, (((1,), (1,)), ((), ())), preferred_element_type=F32)


def _dot_tn(a, b):
    return lax.dot_general(a, b, (((0,), (0,)), ((), ())), preferred_element_type=F32)


def _dot3(a, b_hi, b_lo):
    a_hi, a_lo = _split2(a)
    return _dot(a_hi, b_hi) + _dot(a_lo, b_hi) + _dot(a_hi, b_lo)


def _cond_of_tile(t, tm):
    p_tiles = N_TOK_P // tm
    per_seq = DEC_SEQ // tm
    return jnp.where(t < p_tiles, 0, 1 + (t - p_tiles) // per_seq)


def _mod_spec(layer, chunk, tm):
    return pl.BlockSpec((None, None, None, 1, D_MODEL),
                        lambda t: (layer, _cond_of_tile(t, tm), chunk, 0, 0))


def _ada_body(c_ref, w_ref, b_ref, o_ref):
    s = _silu(c_ref[...]).astype(BF16)
    o_ref[...] = _dot(s, w_ref[...].astype(BF16)) + b_ref[...]


def _ada(cond, ada_w, ada_b):
    tn = 1024
    n_out = N_MOD * D_MODEL
    return pl.pallas_call(
        _ada_body,
        out_shape=jax.ShapeDtypeStruct((DEPTH, COND_ROWS, n_out), F32),
        grid=(DEPTH, n_out // tn),
        in_specs=[pl.BlockSpec((COND_ROWS, D_MODEL), lambda i, n: (0, 0)),
                  pl.BlockSpec((None, D_MODEL, tn), lambda i, n: (i, 0, n)),
                  pl.BlockSpec((None, 1, tn), lambda i, n: (i, 0, n))],
        out_specs=pl.BlockSpec((None, COND_ROWS, tn), lambda i, n: (i, 0, n)),
        compiler_params=_params(("arbitrary", "arbitrary")),
        name="ada",
    )(cond, ada_w, ada_b.reshape(DEPTH, 1, n_out))


def _mm_body(*refs, n_x, n_w, swiglu):
    x_refs = refs[:n_x]
    w_refs = refs[n_x:n_x + n_w]
    o_ref = refs[n_x + n_w]
    wb_refs = refs[n_x + n_w + 1:]

    @pl.when(pl.program_id(1) == 0)
    def _():
        for w_ref, wb in zip(w_refs, wb_refs):
            wb[...] = w_ref[...].astype(BF16)

    if n_x == 1:
        x = x_refs[0][...]
    else:
        x = jnp.concatenate([r[...] for r in x_refs], axis=1)
    accs = [_dot(x, wb[...]) for wb in wb_refs]
    out = _silu(accs[0]) * accs[1] if swiglu else accs[0]
    o_ref[...] = out.astype(o_ref.dtype)


def _matmul(xs, ws, *, layer, n_cols, tn, tm, out_dtype, col0=0, swiglu=False, name):
    m = xs[0].shape[0]
    k = sum(x.shape[1] for x in xs)
    assert m % tm == 0 and n_cols % tn == 0 and col0 % tn == 0 and all(w.shape[1] == k for w in ws)
    cb0 = col0 // tn
    in_specs = [pl.BlockSpec((tm, x.shape[1]), lambda n, r: (r, 0)) for x in xs]
    in_specs += [pl.BlockSpec((None, k, tn), lambda n, r: (layer, 0, cb0 + n)) for _ in ws]
    return pl.pallas_call(
        functools.partial(_mm_body, n_x=len(xs), n_w=len(ws), swiglu=swiglu),
        out_shape=jax.ShapeDtypeStruct((m, n_cols), out_dtype),
        grid=(n_cols // tn, m // tm),
        in_specs=in_specs,
        out_specs=pl.BlockSpec((tm, tn), lambda n, r: (r, n)),
        scratch_shapes=[pltpu.VMEM((k, tn), BF16) for _ in ws],
        compiler_params=_params(("arbitrary", "arbitrary")),
        name=name,
    )(*xs, *ws)


_LN_TM = 256
_P_TILES = N_TOK_P // _LN_TM


def _modulate_body(xp_ref, xs_ref, sc_ref, sh_ref, x_ref, u_ref):
    def emit(x):
        x_ref[...] = x
        u_ref[...] = (x * (1.0 + sc_ref[...]) + sh_ref[...]).astype(BF16)

    @pl.when(pl.program_id(0) < _P_TILES)
    def _():
        emit(xp_ref[...])

    @pl.when(pl.program_id(0) >= _P_TILES)
    def _():
        emit(xs_ref[...])


def _modulate(x_prompt, x_sample, mod5, layer):
    tm = _LN_TM
    row = pl.BlockSpec((tm, D_MODEL), lambda t: (t, 0))
    return pl.pallas_call(
        _modulate_body,
        out_shape=[jax.ShapeDtypeStruct((N_TOK, D_MODEL), F32), jax.ShapeDtypeStruct((N_TOK, D_MODEL), BF16)],
        grid=(N_TOK // tm,),
        in_specs=[pl.BlockSpec((tm, D_MODEL), lambda t: (jnp.minimum(t, _P_TILES - 1), 0)),
                  pl.BlockSpec((tm, D_MODEL), lambda t: (jnp.maximum(t - _P_TILES, 0), 0)),
                  _mod_spec(layer, 1, tm), _mod_spec(layer, 0, tm)],
        out_specs=[row, row],
        compiler_params=_params(("arbitrary",)),
        name="modulate",
    )(x_prompt.reshape(N_TOK_P, D_MODEL), x_sample.reshape(N_TOK_S, D_MODEL), mod5, mod5)


def _ln_body(*refs, modulate, route, gated_pair):
    if gated_pair:
        x_ref, h_ref, h2_ref, pr_ref, g_ref, lg_ref, lb_ref = refs[:7]
        refs = refs[7:]
        pr = pr_ref[...]
        h = h_ref[...].astype(F32) * pr[:, 0:1] + h2_ref[...].astype(F32) * pr[:, 1:2]
    else:
        x_ref, h_ref, g_ref, lg_ref, lb_ref = refs[:5]
        refs = refs[5:]
        h = h_ref[...].astype(F32)
    v = ALPHA * x_ref[...] + g_ref[...] * h
    mu = jnp.mean(v, axis=-1, keepdims=True)
    d = v - mu
    var = jnp.mean(d * d, axis=-1, keepdims=True)
    xn = d * lax.rsqrt(var + EPS) * lg_ref[...] + lb_ref[...]
    if not modulate:
        yp_ref, ys_ref = refs

        @pl.when(pl.program_id(0) < _P_TILES)
        def _():
            yp_ref[...] = xn

        @pl.when(pl.program_id(0) >= _P_TILES)
        def _():
            ys_ref[...] = xn
        return
    sc_ref, sh_ref = refs[:2]
    refs = refs[2:]
    u = xn * (1.0 + sc_ref[...]) + sh_ref[...]
    if not route:
        xo_ref, uo_ref = refs
        xo_ref[...] = xn
        uo_ref[...] = u.astype(BF16)
        return
    r_ref, xo_ref, uo_ref, p_ref, i_ref = refs
    xo_ref[...] = xn
    uo_ref[...] = u.astype(BF16)
    r_hi, r_lo = _split2(r_ref[...])
    logits = _dot3(u, r_hi, r_lo)
    lane = lax.broadcasted_iota(jnp.int32, logits.shape, 1)
    l1 = jnp.where(lane < N_EXPERTS, logits, -jnp.inf)
    m1 = jnp.max(l1, axis=-1, keepdims=True)
    i1 = jnp.min(jnp.where(l1 == m1, lane, LANES), axis=-1, keepdims=True)
    l2 = jnp.where(lane == i1, -jnp.inf, l1)
    m2 = jnp.max(l2, axis=-1, keepdims=True)
    i2 = jnp.min(jnp.where(l2 == m2, lane, LANES), axis=-1, keepdims=True)
    e = jnp.exp(m2 - m1)
    p1 = 1.0 / (1.0 + e)
    p2 = e / (1.0 + e)
    p_ref[...] = jnp.where(lane == 0, p1, jnp.where(lane == 1, p2, 0.0))
    i_ref[...] = jnp.where(lane == 0, i1, jnp.where(lane == 1, i2, 0))


def _ln(x, h, mod5, ln_g, ln_b, *, layer, gate_chunk, mod_next=None, router=None, pair_probs=None):
    tm = _LN_TM
    row = pl.BlockSpec((tm, D_MODEL), lambda t: (t, 0))
    vec = pl.BlockSpec((None, 1, D_MODEL), lambda t: (layer, 0, 0))
    lane_row = pl.BlockSpec((tm, LANES), lambda t: (t, 0))
    if pair_probs is None:
        in_specs = [row, row]
        args = [x, h]
    else:
        in_specs = [row, row, row, lane_row]
        args = [x, h[0], h[1], pair_probs]
    in_specs += [_mod_spec(layer, gate_chunk, tm), vec, vec]
    args += [mod5, ln_g.reshape(DEPTH, 1, D_MODEL), ln_b.reshape(DEPTH, 1, D_MODEL)]
    if mod_next is None:
        out_shape = [jax.ShapeDtypeStruct((N_TOK_P, D_MODEL), F32), jax.ShapeDtypeStruct((N_TOK_S, D_MODEL), F32)]
        out_specs = [pl.BlockSpec((tm, D_MODEL), lambda t: (jnp.minimum(t, _P_TILES - 1), 0)),
                     pl.BlockSpec((tm, D_MODEL), lambda t: (jnp.maximum(t - _P_TILES, 0), 0))]
    else:
        nl, sc_chunk, sh_chunk = mod_next
        in_specs += [_mod_spec(nl, sc_chunk, tm), _mod_spec(nl, sh_chunk, tm)]
        args += [mod5, mod5]
        out_shape = [jax.ShapeDtypeStruct((N_TOK, D_MODEL), F32), jax.ShapeDtypeStruct((N_TOK, D_MODEL), BF16)]
        out_specs = [row, row]
    if router is not None:
        in_specs.append(pl.BlockSpec((D_MODEL, LANES), lambda t: (0, 0)))
        args.append(router)
        out_shape += [jax.ShapeDtypeStruct((N_TOK, LANES), F32), jax.ShapeDtypeStruct((N_TOK, LANES), jnp.int32)]
        out_specs += [lane_row, lane_row]
    return pl.pallas_call(
        functools.partial(_ln_body, modulate=mod_next is not None, route=router is not None,
                          gated_pair=pair_probs is not None),
        out_shape=out_shape,
        grid=(N_TOK // tm,),
        in_specs=in_specs,
        out_specs=out_specs,
        compiler_params=_params(("arbitrary",)),
        name="ln",
    )(*args)


def _rms(x, w):
    return x * lax.rsqrt(jnp.mean(x * x, axis=-1, keepdims=True) + EPS) * w


def _split_body(p_ref, qw_ref, kw_ref, qn_ref, ckv_ref, kr_ref):
    q0 = FOURIER_DIM
    k0 = q0 + Q_RANK
    r0 = k0 + KV_RANK
    qn_ref[...] = _rms(p_ref[:, q0:k0], qw_ref[...]).astype(BF16)
    ckv_ref[...] = _rms(p_ref[:, k0:r0], kw_ref[...])
    kr_ref[...] = p_ref[:, r0:]


def _split_proj(proj, q_norm_w, kv_norm_w, layer, tm=512):
    return pl.pallas_call(
        _split_body,
        out_shape=[jax.ShapeDtypeStruct((N_TOK, Q_RANK), BF16),
                   jax.ShapeDtypeStruct((N_TOK, KV_RANK), F32),
                   jax.ShapeDtypeStruct((N_TOK, ROPE_DIM), F32)],
        grid=(N_TOK // tm,),
        in_specs=[pl.BlockSpec((tm, MIX_IN), lambda t: (t, 0)),
                  pl.BlockSpec((None, 1, Q_RANK), lambda t: (layer, 0, 0)),
                  pl.BlockSpec((None, 1, KV_RANK), lambda t: (layer, 0, 0))],
        out_specs=[pl.BlockSpec((tm, Q_RANK), lambda t: (t, 0)),
                   pl.BlockSpec((tm, KV_RANK), lambda t: (t, 0)),
                   pl.BlockSpec((tm, ROPE_DIM), lambda t: (t, 0))],
        compiler_params=_params(("arbitrary",)),
        name="split_proj",
    )(proj, q_norm_w.reshape(N_EVEN, 1, Q_RANK), kv_norm_w.reshape(N_EVEN, 1, KV_RANK))


def _bf16_pair(a):
    hi = a.astype(np.float32).astype(jnp.bfloat16)
    lo = (a - hi.astype(np.float64)).astype(np.float32).astype(jnp.bfloat16)
    return np.stack([hi, lo])


@functools.lru_cache(maxsize=None)
def _dft_tables(length):
    k = np.arange(length)
    ang = 2.0 * np.pi * (np.outer(k, k) % length) / length
    scale = 1.0 / np.sqrt(length * FOURIER_GROUP_DIM)
    t_len = np.concatenate([np.cos(ang), -np.sin(ang)], axis=1) * scale
    c = np.arange(FOURIER_GROUP_DIM)
    angc = 2.0 * np.pi * (np.outer(c, c) % FOURIER_GROUP_DIM) / FOURIER_GROUP_DIM
    eye = np.eye(FOURIER_GROUPS)
    t_ch = np.concatenate([np.kron(eye, np.cos(angc)), np.kron(eye, np.sin(angc))], axis=1)
    return _bf16_pair(t_len), _bf16_pair(t_ch)


def _fourier_body(f_ref, tl_ref, tc_ref, *rest):
    o_ref = rest[-1]
    gh = _dot3(f_ref[...], tc_ref[0], tc_ref[1])
    stacked = jnp.concatenate([gh[:, :FOURIER_DIM], gh[:, FOURIER_DIM:]], axis=0)
    s_hi, s_lo = _split2(stacked)
    y = _dot(tl_ref[0], s_hi) + _dot(tl_ref[1], s_hi) + _dot(tl_ref[0], s_lo)
    o_ref[...] = y.astype(BF16)


def _fourier(proj, n_seq, length, row0, prev=None):
    t_len, t_ch = _dft_tables(length)
    blk0 = row0 // length
    in_specs = [pl.BlockSpec((length, FOURIER_DIM), lambda b: (blk0 + b, 0)),
                pl.BlockSpec((2, length, 2 * length), lambda b: (0, 0, 0)),
                pl.BlockSpec((2, FOURIER_DIM, 2 * FOURIER_DIM), lambda b: (0, 0, 0))]
    args = [proj, jnp.asarray(t_len), jnp.asarray(t_ch)]
    aliases = {}
    if prev is not None:
        in_specs.append(ANY_SPEC)
        args.append(prev)
        aliases = {3: 0}
    return pl.pallas_call(
        _fourier_body,
        out_shape=jax.ShapeDtypeStruct((N_TOK, FOURIER_DIM), BF16),
        grid=(n_seq,),
        in_specs=in_specs,
        out_specs=pl.BlockSpec((length, FOURIER_DIM), lambda b: (blk0 + b, 0)),
        input_output_aliases=aliases,
        compiler_params=_params(("arbitrary",)),
        name="fourier",
    )(*args)


def _rope_pair(x, cos, sin_signed):
    lane = lax.broadcasted_iota(jnp.int32, x.shape, 1)
    first_half = (lane & 31) < 16
    partner = jnp.where(first_half, pltpu.roll(x, LANES - 16, 1), pltpu.roll(x, 16, 1))
    return x * cos + partner * sin_signed


def _attn_body(*refs, rope, ctx):
    q_ref, kv_ref, kr_ref = refs[:3]
    refs = refs[3:]
    if rope:
        cq_ref, sq_ref, ck_ref, sk_ref = refs[:4]
        refs = refs[4:]
    if ctx:
        kvc_ref, krc_ref = refs[:2]
        refs = refs[2:]
    o_ref = refs[-1]

    kr = kr_ref[...]
    if rope:
        kr = _rope_pair(jnp.concatenate([kr, kr], axis=1), ck_ref[...], sk_ref[...])[:, :ROPE_DIM]
    kr = kr.astype(BF16)
    if ctx:
        krc = krc_ref[...].astype(BF16)
    rope0 = MLA_HEADS * NOPE_DIM
    for pr in range(MLA_HEADS // 2):
        qr = q_ref[:, rope0 + pr * LANES:rope0 + (pr + 1) * LANES]
        if rope:
            qr = _rope_pair(qr, cq_ref[...], sq_ref[...])
        qr = (qr * ATTN_SCALE).astype(BF16)
        for hh in range(2):
            h = 2 * pr + hh
            qn = (q_ref[:, h * NOPE_DIM:(h + 1) * NOPE_DIM] * ATTN_SCALE).astype(BF16)
            qrh = qr[:, hh * ROPE_DIM:(hh + 1) * ROPE_DIM]
            c0 = h * (NOPE_DIM + V_DIM)
            s = _dot_nt(qn, kv_ref[:, c0:c0 + NOPE_DIM]) + _dot_nt(qrh, kr)
            m = jnp.max(s, axis=-1, keepdims=True)
            if ctx:
                sc = _dot_nt(qn, kvc_ref[:, c0:c0 + NOPE_DIM]) + _dot_nt(qrh, krc)
                m = jnp.maximum(m, jnp.max(sc, axis=-1, keepdims=True))
            p = jnp.exp(s - m)
            den = jnp.sum(p, axis=-1, keepdims=True)
            acc = _dot(p.astype(BF16), kv_ref[:, c0 + NOPE_DIM:c0 + NOPE_DIM + V_DIM])
            if ctx:
                pc = jnp.exp(sc - m)
                den = den + jnp.sum(pc, axis=-1, keepdims=True)
                acc = acc + _dot(pc.astype(BF16), kvc_ref[:, c0 + NOPE_DIM:c0 + NOPE_DIM + V_DIM])
            o_ref[:, h * V_DIM:(h + 1) * V_DIM] = (acc / den).astype(BF16)


def _attention(q, kv, kr, *, n_seq, length, row0, tq, rope_tabs=None, ctx=None, prev=None):
    n_qt = length // tq
    qblk0 = row0 // tq
    kblk0 = row0 // length
    kv_w = MLA_HEADS * (NOPE_DIM + V_DIM)
    in_specs = [pl.BlockSpec((tq, MLA_HEADS * QK_DIM), lambda b, i: (qblk0 + b * n_qt + i, 0)),
                pl.BlockSpec((length, kv_w), lambda b, i: (kblk0 + b, 0)),
                pl.BlockSpec((length, ROPE_DIM), lambda b, i: (kblk0 + b, 0))]
    args = [q, kv, kr]
    if rope_tabs is not None:
        cos, sin = rope_tabs
        in_specs += [pl.BlockSpec((tq, LANES), lambda b, i: (i, 0)),
                     pl.BlockSpec((tq, LANES), lambda b, i: (i, 0)),
                     pl.BlockSpec((length, LANES), lambda b, i: (0, 0)),
                     pl.BlockSpec((length, LANES), lambda b, i: (0, 0))]
        args += [cos, sin, cos, sin]
    if ctx is not None:
        krc, cblk0 = ctx
        in_specs += [pl.BlockSpec((PAST_LEN, kv_w), lambda b, i: (cblk0 + b, 0)),
                     pl.BlockSpec((None, PAST_LEN, ROPE_DIM), lambda b, i: (b, 0, 0))]
        args += [kv, krc]
    aliases = {}
    if prev is not None:
        aliases = {len(args): 0}
        in_specs.append(ANY_SPEC)
        args.append(prev)
    return pl.pallas_call(
        functools.partial(_attn_body, rope=rope_tabs is not None, ctx=ctx is not None),
        out_shape=jax.ShapeDtypeStruct((N_TOK, MLA_HEADS * V_DIM), BF16),
        grid=(n_seq, n_qt),
        in_specs=in_specs,
        out_specs=pl.BlockSpec((tq, MLA_HEADS * V_DIM), lambda b, i: (qblk0 + b * n_qt + i, 0)),
        input_output_aliases=aliases,
        compiler_params=_params(("arbitrary", "arbitrary")),
        name="attention",
    )(*args)


def _rope_tables():
    rows = DEC_SEQ // GRID_W
    row = jnp.repeat(jnp.arange(rows, dtype=F32), GRID_W)
    col = jnp.tile(jnp.arange(GRID_W, dtype=F32), rows)
    inv = ROPE_THETA ** (-jnp.arange(ROPE_DIM // 4, dtype=F32) * 2.0 / (ROPE_DIM // 2))
    ang = jnp.stack([row[:, None] * inv, col[:, None] * inv], axis=1)
    cos, sin = jnp.cos(ang), jnp.sin(ang)
    cos64 = jnp.stack([cos, cos], axis=2).reshape(DEC_SEQ, ROPE_DIM)
    sin64 = jnp.stack([-sin, sin], axis=2).reshape(DEC_SEQ, ROPE_DIM)
    return jnp.tile(cos64, (1, 2)), jnp.tile(sin64, (1, 2))


_CONV_PAD = 8


@functools.lru_cache(maxsize=None)
def _head_expand():
    e = np.zeros((2, LANES, GROUP_DIM), np.float32)
    for h in range(HEADS_PER_GROUP):
        e[0, h, h * SSM_HEAD_DIM:(h + 1) * SSM_HEAD_DIM] = 1.0
        e[1, HEADS_PER_GROUP + h, h * SSM_HEAD_DIM:(h + 1) * SSM_HEAD_DIM] = 1.0
    return e.astype(jnp.bfloat16)


def _scan_body(*refs, length, zero_init, emit_state, n_prev):
    (xr_ref, br_ref, cr_ref, wx_ref, wb_ref, wc_ref, bx_ref, bb_ref, bc_ref,
     dt_ref, bias_ref, alog_ref, dskip_ref, e_ref) = refs[:14]
    refs = refs[14:]
    if not zero_init:
        h0f_ref, h0b_ref = refs[:2]
        refs = refs[2:]
    refs = refs[n_prev:]
    y_ref = refs[0]
    refs = refs[1:]
    if emit_state:
        hf_ref, hb_ref = refs[:2]
        refs = refs[2:]
    padx, padb, padc, y_scr, bm_scr, cm_scr, xwf_scr, xwb_scr, decf_scr, decb_scr = refs
    q = SSM_CHUNK
    n_chunks = length // q
    hpg = HEADS_PER_GROUP
    hd = SSM_HEAD_DIM
    state_shape = (GROUP_DIM, SSM_STATE)

    for raw, pad in ((xr_ref, padx), (br_ref, padb), (cr_ref, padc)):
        zeros = jnp.zeros((_CONV_PAD, pad.shape[1]), F32)
        pad[0:_CONV_PAD, :] = zeros
        pad[_CONV_PAD + length:, :] = zeros
        pad[_CONV_PAD:_CONV_PAD + length, :] = raw[...].astype(F32)

    rr = lax.broadcasted_iota(jnp.int32, (q, q), 0)
    cc = lax.broadcasted_iota(jnp.int32, (q, q), 1)
    lower = cc <= rr
    upper = cc >= rr
    tri = lower.astype(BF16)
    pair_lane = lax.broadcasted_iota(jnp.int32, (q, 2 * hd), 1)
    a_neg = -jnp.exp(alog_ref[...])
    bias = bias_ref[...]
    e_f = e_ref[0]
    e_b = e_ref[1]

    def conv_silu(pad, w_ref, b_ref, r0):
        acc = b_ref[...]
        for k in range(SSM_CONV):
            off = r0 + _CONV_PAD + k - SSM_CONV // 2
            acc = acc + w_ref[k:k + 1, :] * pad[off:off + q, :]
        return _silu(acc)

    totals = []
    for c in range(n_chunks):
        r0 = c * q
        rows = slice(r0, r0 + q)
        x = conv_silu(padx, wx_ref, bx_ref, r0)
        bm = conv_silu(padb, wb_ref, bb_ref, r0).astype(BF16)
        cm = conv_silu(padc, wc_ref, bc_ref, r0).astype(BF16)
        bm_scr[rows, :] = bm
        cm_scr[rows, :] = cm
        t = dt_ref[rows, :] + bias
        dt = jnp.maximum(t, 0.0) + jnp.log1p(jnp.exp(-jnp.abs(t)))
        a = dt * a_neg
        a_hi = a.astype(BF16)
        r1 = a - a_hi.astype(F32)
        a_mid = r1.astype(BF16)
        a_lo = (r1 - a_mid.astype(F32)).astype(BF16)
        cs = _dot(tri, a_hi) + _dot(tri, a_mid) + _dot(tri, a_lo)
        ecs = cs - a
        total = cs[q - 1:q, :]
        totals.append(jnp.exp(total))
        cs_t, ecs_t, dt_t = cs.T, ecs.T, dt.T
        cb = _dot_nt(cm, bm)
        pieces = []
        for pr in range(hpg // 2):
            mats = []
            for h in (2 * pr, 2 * pr + 1):
                lf = jnp.exp(jnp.where(lower, cs[:, h:h + 1] - cs_t[h:h + 1, :], NEG_BIG))
                lb = jnp.exp(jnp.where(upper, ecs_t[hpg + h:hpg + h + 1, :] - ecs[:, hpg + h:hpg + h + 1], NEG_BIG))
                mats.append((cb * (lf * dt_t[h:h + 1, :] + lb * dt_t[hpg + h:hpg + h + 1, :])).astype(BF16))
            xp = x[:, pr * 2 * hd:(pr + 1) * 2 * hd]
            rhs = jnp.concatenate([jnp.where(pair_lane < hd, xp, 0.0), jnp.where(pair_lane >= hd, xp, 0.0)], axis=0)
            pieces.append(_dot(jnp.concatenate(mats, axis=1), rhs.astype(BF16)))
        y_scr[rows, :] = jnp.concatenate(pieces, axis=1) + dskip_ref[...] * x
        decf_scr[rows, :] = _dot(jnp.exp(cs).astype(BF16), e_f).astype(BF16)
        decb_scr[rows, :] = _dot(jnp.exp(total - ecs).astype(BF16), e_b).astype(BF16)
        xwf_scr[rows, :] = (x * _dot((jnp.exp(total - cs) * dt).astype(BF16), e_f)).astype(BF16)
        xwb_scr[rows, :] = (x * _dot((jnp.exp(ecs) * dt).astype(BF16), e_b)).astype(BF16)

    def step(state, c, dec_scr, xw_scr, lane0):
        rows = slice(c * q, (c + 1) * q)
        y_off = _dot_nt(cm_scr[rows, :], state.astype(BF16))
        y_scr[rows, :] = y_scr[rows, :] + y_off * dec_scr[rows, :].astype(F32)
        upd = _dot_tn(xw_scr[rows, :], bm_scr[rows, :])
        decay = totals[c]
        return jnp.concatenate(
            [state[h * hd:(h + 1) * hd, :] * decay[:, lane0 + h:lane0 + h + 1] + upd[h * hd:(h + 1) * hd, :]
             for h in range(hpg)], axis=0)

    if zero_init:
        hf = jnp.zeros(state_shape, F32)
        hb = jnp.zeros(state_shape, F32)
    else:
        hf = h0f_ref[...].reshape(state_shape)
        hb = h0b_ref[...].reshape(state_shape)
    for i in range(n_chunks):
        hf = step(hf, i, decf_scr, xwf_scr, 0)
        hb = step(hb, n_chunks - 1 - i, decb_scr, xwb_scr, hpg)
    y_ref[...] = y_scr[...].astype(BF16)
    if emit_state:
        hf_ref[...] = hf.reshape(hf_ref.shape)
        hb_ref[...] = hb.reshape(hb_ref.shape)


def _scan(xbc, dtg, conv_w, conv_b, bias_g, alog_g, dskip, layer, *, n_seq, length, row0, h0=None,
          y_prev=None, state_prev=None, emit_state):
    blk0 = row0 // length
    b0 = SSM_INNER // SSM_STATE
    c0 = b0 + SSM_GROUPS
    seq = lambda width, col: pl.BlockSpec((length, width), lambda b, g: (blk0 + b, col + g))
    cw = lambda width, col: pl.BlockSpec((None, SSM_CONV, width), lambda b, g: (layer, 0, col + g))
    cbias = lambda width, col: pl.BlockSpec((None, 1, width), lambda b, g: (layer, 0, col + g))
    grp = lambda width: pl.BlockSpec((None, 1, width), lambda b, g: (layer, 0, g))
    h0_spec = pl.BlockSpec((None, None, HEADS_PER_GROUP, SSM_HEAD_DIM, SSM_STATE), lambda b, g: (b, layer, g, 0, 0))
    in_specs = [seq(GROUP_DIM, 0), seq(SSM_STATE, b0), seq(SSM_STATE, c0),
                cw(GROUP_DIM, 0), cw(SSM_STATE, b0), cw(SSM_STATE, c0),
                cbias(GROUP_DIM, 0), cbias(SSM_STATE, b0), cbias(SSM_STATE, c0),
                seq(LANES, 0), grp(LANES), grp(LANES), grp(GROUP_DIM),
                pl.BlockSpec((2, LANES, GROUP_DIM), lambda b, g: (0, 0, 0))]
    cb3 = conv_b.reshape(N_ODD, 1, SSM_CONV_DIM)
    args = [xbc, xbc, xbc, conv_w, conv_w, conv_w, cb3, cb3, cb3, dtg, bias_g, alog_g, dskip,
            jnp.asarray(_head_expand())]
    if h0 is not None:
        in_specs += [h0_spec, h0_spec]
        args += list(h0)
    out_shape = [jax.ShapeDtypeStruct((N_TOK, SSM_INNER), BF16)]
    out_specs = [seq(GROUP_DIM, 0)]
    if emit_state:
        st = jax.ShapeDtypeStruct((n_seq, N_ODD, SSM_HEADS, SSM_HEAD_DIM, SSM_STATE), F32)
        out_shape += [st, st]
        out_specs += [h0_spec, h0_spec]
    aliases = {}
    prevs = ([] if y_prev is None else [y_prev]) + ([] if state_prev is None else list(state_prev))
    for k, prev in enumerate(prevs):
        aliases[len(args)] = k if y_prev is not None else k + 1
        in_specs.append(ANY_SPEC)
        args.append(prev)
    pad_rows = length + 2 * _CONV_PAD
    scratch = [pltpu.VMEM((pad_rows, GROUP_DIM), F32), pltpu.VMEM((pad_rows, SSM_STATE), F32),
               pltpu.VMEM((pad_rows, SSM_STATE), F32), pltpu.VMEM((length, GROUP_DIM), F32),
               pltpu.VMEM((length, SSM_STATE), BF16), pltpu.VMEM((length, SSM_STATE), BF16),
               pltpu.VMEM((length, GROUP_DIM), BF16), pltpu.VMEM((length, GROUP_DIM), BF16),
               pltpu.VMEM((length, GROUP_DIM), BF16), pltpu.VMEM((length, GROUP_DIM), BF16)]
    return pl.pallas_call(
        functools.partial(_scan_body, length=length, zero_init=h0 is None, emit_state=emit_state,
                          n_prev=len(prevs)),
        out_shape=out_shape,
        grid=(n_seq, SSM_GROUPS),
        in_specs=in_specs,
        out_specs=out_specs,
        scratch_shapes=scratch,
        input_output_aliases=aliases,
        compiler_params=_params(("arbitrary", "arbitrary")),
        name="ssd_scan",
    )(*args)


def _gate_norm_body(y_ref, z_ref, w_ref, o_ref):
    g = y_ref[...].astype(F32) * _silu(z_ref[...].astype(F32))
    o_ref[...] = _rms(g, w_ref[...]).astype(BF16)


def _gate_norm(y, z, norm_w, layer, tm=256):
    return pl.pallas_call(
        _gate_norm_body,
        out_shape=jax.ShapeDtypeStruct((N_TOK, SSM_INNER), BF16),
        grid=(N_TOK // tm,),
        in_specs=[pl.BlockSpec((tm, SSM_INNER), lambda t: (t, 0)),
                  pl.BlockSpec((tm, SSM_INNER), lambda t: (t, 0)),
                  pl.BlockSpec((None, 1, SSM_INNER), lambda t: (layer, 0, 0))],
        out_specs=pl.BlockSpec((tm, SSM_INNER), lambda t: (t, 0)),
        compiler_params=_params(("arbitrary",)),
        name="gate_norm",
    )(y, z, norm_w.reshape(N_ODD, 1, SSM_INNER))


def _moe_up_body(e_ref, n_ref, m_ref, first_ref, valid_ref, x_ref, wg_ref, wu_ref, o_ref, gb, ub):
    s = pl.program_id(0)

    @pl.when(first_ref[s] == 1)
    def _():
        gb[...] = wg_ref[...].astype(BF16)
        ub[...] = wu_ref[...].astype(BF16)

    @pl.when(valid_ref[s] == 1)
    def _():
        x = x_ref[...]
        o_ref[...] = (_silu(_dot(x, gb[...])) * _dot(x, ub[...])).astype(BF16)


def _moe_down_body(e_ref, n_ref, m_ref, first_ref, valid_ref, h_ref, w_ref, o_ref, wb):
    s = pl.program_id(0)

    @pl.when(first_ref[s] == 1)
    def _():
        wb[...] = w_ref[...].astype(BF16)

    @pl.when(valid_ref[s] == 1)
    def _():
        o_ref[...] = _dot(h_ref[...], wb[...]).astype(BF16)


def _moe_tables(tiles, n_col_tiles):
    n_steps = MOE_TILES * n_col_tiles
    tile_start = jnp.cumsum(tiles) - tiles
    step_end = jnp.cumsum(tiles * n_col_tiles)
    total = step_end[-1]
    s = jnp.minimum(jnp.arange(n_steps, dtype=jnp.int32), total - 1)
    e = jnp.sum((s[:, None] >= step_end[None, :]).astype(jnp.int32), axis=1)
    local = s - (step_end - tiles * n_col_tiles)[e]
    te = jnp.maximum(tiles[e], 1)
    n = local // te
    k = local - n * te
    m = tile_start[e] + k
    valid = (jnp.arange(n_steps) < total).astype(jnp.int32)
    first = ((k == 0) & (valid == 1)).astype(jnp.int32)
    return e.astype(jnp.int32), n.astype(jnp.int32), m.astype(jnp.int32), first, valid


def _moe(u, idx, w_gate, w_up, w_down, layer):
    tm = MOE_TM
    flat_e = idx.reshape(-1)
    onehot = (flat_e[:, None] == jnp.arange(N_EXPERTS)[None, :]).astype(jnp.int32)
    rank = jnp.sum((jnp.cumsum(onehot, axis=0) - onehot) * onehot, axis=1)
    count = jnp.sum(onehot, axis=0)
    tiles = (count + tm - 1) // tm
    row_start = (jnp.cumsum(tiles) - tiles) * tm
    dest = row_start[flat_e] + rank
    token = jnp.arange(N_TOK * TOP_K, dtype=jnp.int32) // TOP_K
    src = jnp.zeros((MOE_ROWS,), jnp.int32).at[dest].set(token, mode="promise_in_bounds", unique_indices=True)
    x_sorted = u.at[src].get(mode="promise_in_bounds")

    tn = 1024
    n_up = EXPERT_DIM // tn
    tabs = _moe_tables(tiles, n_up)
    w_spec = pl.BlockSpec((None, None, D_MODEL, tn), lambda s, e, n, m, f, v: (layer, e[s], 0, n[s]))
    hmid = pl.pallas_call(
        _moe_up_body,
        out_shape=jax.ShapeDtypeStruct((MOE_ROWS, EXPERT_DIM), BF16),
        grid_spec=pltpu.PrefetchScalarGridSpec(
            num_scalar_prefetch=5,
            grid=(MOE_TILES * n_up,),
            in_specs=[pl.BlockSpec((tm, D_MODEL), lambda s, e, n, m, f, v: (m[s], 0)), w_spec, w_spec],
            out_specs=pl.BlockSpec((tm, tn), lambda s, e, n, m, f, v: (m[s], n[s])),
            scratch_shapes=[pltpu.VMEM((D_MODEL, tn), BF16), pltpu.VMEM((D_MODEL, tn), BF16)]),
        compiler_params=_params(("arbitrary",)),
        name="moe_up",
    )(*tabs, x_sorted, w_gate, w_up)

    tn = 512
    n_dn = D_MODEL // tn
    tabs = _moe_tables(tiles, n_dn)
    y_sorted = pl.pallas_call(
        _moe_down_body,
        out_shape=jax.ShapeDtypeStruct((MOE_ROWS, D_MODEL), BF16),
        grid_spec=pltpu.PrefetchScalarGridSpec(
            num_scalar_prefetch=5,
            grid=(MOE_TILES * n_dn,),
            in_specs=[pl.BlockSpec((tm, EXPERT_DIM), lambda s, e, n, m, f, v: (m[s], 0)),
                      pl.BlockSpec((None, None, EXPERT_DIM, tn), lambda s, e, n, m, f, v: (layer, e[s], 0, n[s]))],
            out_specs=pl.BlockSpec((tm, tn), lambda s, e, n, m, f, v: (m[s], n[s])),
            scratch_shapes=[pltpu.VMEM((EXPERT_DIM, tn), BF16)]),
        compiler_params=_params(("arbitrary",)),
        name="moe_down",
    )(*tabs, hmid, w_down)

    dest2 = dest.reshape(N_TOK, TOP_K)
    return (y_sorted.at[dest2[:, 0]].get(mode="promise_in_bounds"),
            y_sorted.at[dest2[:, 1]].get(mode="promise_in_bounds"))


def _q_perm():
    h = np.arange(MLA_HEADS)[:, None]
    nope = (h * QK_DIM + np.arange(NOPE_DIM)[None, :]).reshape(-1)
    rope = (h * QK_DIM + NOPE_DIM + np.arange(ROPE_DIM)[None, :]).reshape(-1)
    return np.concatenate([nope, rope])


def _even_mixer(u, j, p, cache_ckv, cache_krope, rope_tabs):
    proj = _matmul([u], [p['mix_w_in']], layer=j, n_cols=MIX_IN, tn=MIX_IN, tm=512, out_dtype=F32, name="mix_in")
    qn, ckv, kr = _split_proj(proj, p['q_norm_w'], p['kv_norm_w'], j)
    f = _fourier(proj, BATCH, SEQ, 0)
    f = _fourier(proj, DEC_BATCH, DEC_SEQ, N_TOK_P, prev=f)
    q = _matmul([qn], [p['w_q_perm']], layer=j, n_cols=MLA_HEADS * QK_DIM, tn=768, tm=1024, out_dtype=F32,
                name="q_proj")
    ckv_all = jnp.concatenate([ckv, cache_ckv[:, j].reshape(DEC_BATCH * PAST_LEN, KV_RANK)], axis=0).astype(BF16)
    kv = _matmul([ckv_all], [p['w_kv_b']], layer=j, n_cols=MLA_HEADS * (NOPE_DIM + V_DIM), tn=1024, tm=1024,
                 out_dtype=BF16, name="kv_proj")
    o = _attention(q, kv, kr, n_seq=BATCH, length=SEQ, row0=0, tq=SEQ)
    o = _attention(q, kv, kr, n_seq=DEC_BATCH, length=DEC_SEQ, row0=N_TOK_P, tq=256, rope_tabs=rope_tabs,
                   ctx=(cache_krope[:, j], N_TOK // PAST_LEN), prev=o)
    h = _matmul([f, o], [p['mix_w_out']], layer=j, n_cols=D_MODEL, tn=1024, tm=1024, out_dtype=BF16, name="mix_out")
    return h, ckv, kr


def _group_dt_columns(w):
    lead = w.shape[:-1]
    w = w.reshape(*lead, 2, SSM_GROUPS, HEADS_PER_GROUP)
    w = jnp.moveaxis(w, -3, -2).reshape(*lead, SSM_GROUPS, 2 * HEADS_PER_GROUP)
    pad = [(0, 0)] * (w.ndim - 1) + [(0, LANES - 2 * HEADS_PER_GROUP)]
    return jnp.pad(w, pad).reshape(*lead, SSM_GROUPS * LANES)


def _odd_mixer(u, j, p, state_f, state_b, state_prev):
    z = _matmul([u], [p['ssm_w_in']], layer=j, n_cols=SSM_INNER, tn=1024, tm=1024, out_dtype=BF16, name="ssm_in_z")
    xbc = _matmul([u], [p['ssm_w_in']], layer=j, n_cols=SSM_CONV_DIM, col0=SSM_INNER, tn=1024, tm=1024,
                  out_dtype=BF16, name="ssm_in_xbc")
    dtg = _matmul([u], [p['ssm_w_dtg']], layer=j, n_cols=SSM_GROUPS * LANES, tn=512, tm=1024, out_dtype=F32,
                  name="ssm_dt")
    common = (xbc, dtg, p['ssm_conv_w'], p['ssm_conv_b'], p['ssm_bias_g'], p['ssm_alog_g'], p['ssm_dskip'], j)
    y, hf, hb = _scan(*common, n_seq=BATCH, length=SEQ, row0=0, state_prev=state_prev, emit_state=True)
    (y,) = _scan(*common, n_seq=DEC_BATCH, length=DEC_SEQ, row0=N_TOK_P, h0=(state_f, state_b), y_prev=y,
                 emit_state=False)
    yn = _gate_norm(y, z, p['ssm_norm_w'], j)
    h = _matmul([yn], [p['ssm_w_out']], layer=j, n_cols=D_MODEL, tn=512, tm=1024, out_dtype=BF16, name="ssm_out")
    return h, hf, hb


def kernel(x_prompt, x_sample, cache_ckv, cache_krope, state_ssm_fwd, state_ssm_bwd, c, c_ctx, ada_w, ada_b, ln1_g, ln1_b, ln2_g, ln2_b, mix_w_in, q_norm_w, w_q_b, kv_norm_w, w_kv_b, mix_w_out, ffn_w_gate, ffn_w_up, ffn_w_down, ssm_w_in, ssm_conv_w, ssm_conv_b, ssm_dt_bias, ssm_a_log, ssm_d, ssm_norm_w, ssm_w_out, moe_router, moe_w_gate, moe_w_up, moe_w_down):
    p = dict(mix_w_in=mix_w_in, q_norm_w=q_norm_w, w_q_perm=w_q_b[:, :, _q_perm()], kv_norm_w=kv_norm_w,
             w_kv_b=w_kv_b, mix_w_out=mix_w_out, ssm_w_in=ssm_w_in, ssm_conv_w=ssm_conv_w, ssm_conv_b=ssm_conv_b,
             ssm_w_dtg=_group_dt_columns(ssm_w_in[:, :, SSM_INNER + SSM_CONV_DIM:]),
             ssm_bias_g=_group_dt_columns(ssm_dt_bias.reshape(N_ODD, 1, 2 * SSM_HEADS)),
             ssm_alog_g=_group_dt_columns(ssm_a_log.reshape(N_ODD, 1, 2 * SSM_HEADS)),
             ssm_dskip=jnp.repeat(ssm_d, SSM_HEAD_DIM, axis=1).reshape(N_ODD, 1, SSM_INNER),
             ssm_norm_w=ssm_norm_w, ssm_w_out=ssm_w_out)
    cond = jnp.concatenate([c_ctx[None, :], c, jnp.zeros((COND_ROWS - N_COND, D_MODEL), F32)], axis=0)
    mod = _ada(cond, ada_w, ada_b)
    mod5 = mod[:, :N_COND].reshape(DEPTH, N_COND, N_MOD, 1, D_MODEL)
    rope_tabs = _rope_tables()

    x, u = _modulate(x_prompt, x_sample, mod5, 0)
    ckvs, krs = [], []
    states = None
    for i in range(DEPTH):
        j = i // 2
        if i % 2 == 0:
            h, ckv, kr = _even_mixer(u, j, p, cache_ckv, cache_krope, rope_tabs)
            ckvs.append(ckv[:N_TOK_P].reshape(BATCH, SEQ, KV_RANK))
            krs.append(kr[:N_TOK_P].reshape(BATCH, SEQ, ROPE_DIM))
            x, u = _ln(x, h, mod5, ln1_g, ln1_b, layer=i, gate_chunk=2, mod_next=(i, 4, 3))
            hmid = _matmul([u], [ffn_w_gate, ffn_w_up], layer=j, n_cols=FFN_DIM, tn=512, tm=1024, out_dtype=BF16,
                           swiglu=True, name="ffn_up")
            f = _matmul([hmid], [ffn_w_down], layer=j, n_cols=D_MODEL, tn=512, tm=512, out_dtype=BF16,
                        name="ffn_down")
            pair_probs = None
        else:
            h, hf, hb = _odd_mixer(u, j, p, state_ssm_fwd, state_ssm_bwd, states)
            states = (hf, hb)
            router = jnp.pad(moe_router[j], ((0, 0), (0, LANES - N_EXPERTS)))
            x, u, pair_probs, idx = _ln(x, h, mod5, ln1_g, ln1_b, layer=i, gate_chunk=2, mod_next=(i, 4, 3),
                                        router=router)
            f = _moe(u, idx[:, :TOP_K], moe_w_gate, moe_w_up, moe_w_down, j)
        if i + 1 < DEPTH:
            x, u = _ln(x, f, mod5, ln2_g, ln2_b, layer=i, gate_chunk=5, mod_next=(i + 1, 1, 0),
                       pair_probs=pair_probs)
        else:
            y_prompt, y_sample = _ln(x, f, mod5, ln2_g, ln2_b, layer=i, gate_chunk=5, pair_probs=pair_probs)
    return (y_prompt.reshape(BATCH, SEQ, D_MODEL), y_sample.reshape(DEC_BATCH, DEC_SEQ, D_MODEL),
            jnp.stack(ckvs, axis=1), jnp.stack(krs, axis=1), states[0], states[1])
```

```python
import functools

import numpy as np
import jax
import jax.numpy as jnp
from jax import lax
from jax.experimental import pallas as pl
from jax.experimental.pallas import tpu as pltpu

D_MODEL = 2048
BATCH = 16
SEQ = 256
DEPTH = 4
DEC_BATCH = 2
DEC_SEQ = 1024
PAST_LEN = 512
GRID_W = 64
N_EVEN = (DEPTH + 1) // 2
N_ODD = DEPTH // 2
FOURIER_GROUPS = 4
FOURIER_GROUP_DIM = 128
FOURIER_DIM = FOURIER_GROUPS * FOURIER_GROUP_DIM
MLA_HEADS = 12
Q_RANK = 768
KV_RANK = 256
NOPE_DIM = 128
ROPE_DIM = 64
V_DIM = 128
QK_DIM = NOPE_DIM + ROPE_DIM
ATTN_SCALE = QK_DIM ** -0.5
ROPE_THETA = 10000.0
MIX_IN = FOURIER_DIM + Q_RANK + KV_RANK + ROPE_DIM
SSM_INNER = 2 * D_MODEL
SSM_HEAD_DIM = 64
SSM_HEADS = SSM_INNER // SSM_HEAD_DIM
SSM_GROUPS = 8
SSM_STATE = 128
SSM_CONV = 5
SSM_CHUNK = 128
SSM_CONV_DIM = SSM_INNER + 2 * SSM_GROUPS * SSM_STATE
FFN_DIM = 5632
N_EXPERTS = 8
TOP_K = 2
EXPERT_DIM = 4096
ALPHA = (2 * DEPTH) ** 0.25
EPS = 1e-5

F32 = jnp.float32
BF16 = jnp.bfloat16

N_TOK_P = BATCH * SEQ
N_TOK_S = DEC_BATCH * DEC_SEQ
N_TOK = N_TOK_P + N_TOK_S
N_MOD = 6
N_COND = 1 + DEC_BATCH
COND_ROWS = 8
HEADS_PER_GROUP = SSM_HEADS // SSM_GROUPS
GROUP_DIM = HEADS_PER_GROUP * SSM_HEAD_DIM
LANES = 128
V7X_VMEM_LIMIT = 52 * 1024 * 1024
NEG_BIG = -1e30

MOE_TM = 512
MOE_TILES = (N_TOK * TOP_K) // MOE_TM + N_EXPERTS
MOE_ROWS = MOE_TILES * MOE_TM

ANY_SPEC = pl.BlockSpec(memory_space=pl.ANY)


def _params(sem, vmem_limit=V7X_VMEM_LIMIT):
    return pltpu.CompilerParams(dimension_semantics=sem, vmem_limit_bytes=vmem_limit)


def _silu(x):
    return x * jax.nn.sigmoid(x)


def _split2(x):
    hi = x.astype(BF16)
    lo = (x - hi.astype(F32)).astype(BF16)
    return hi, lo


def _dot(a, b):
    return jnp.dot(a, b, preferred_element_type=F32)


def _dot_nt(a, b):
    return lax.dot_general(a, b, (((1,), (1,)), ((), ())), preferred_element_type=F32)


def _dot_tn(a, b):
    return lax.dot_general(a, b, (((0,), (0,)), ((), ())), preferred_element_type=F32)


def _dot3(a, b_hi, b_lo):
    a_hi, a_lo = _split2(a)
    return _dot(a_hi, b_hi) + _dot(a_lo, b_hi) + _dot(a_hi, b_lo)


def _cond_of_tile(t, tm):
    p_tiles = N_TOK_P // tm
    per_seq = DEC_SEQ // tm
    return jnp.where(t < p_tiles, 0, 1 + (t - p_tiles) // per_seq)


def _mod_spec(layer, chunk, tm):
    return pl.BlockSpec((None, None, None, 1, D_MODEL),
                        lambda t: (layer, _cond_of_tile(t, tm), chunk, 0, 0))


def _ada_body(c_ref, w_ref, b_ref, o_ref):
    s = _silu(c_ref[...]).astype(BF16)
    o_ref[...] = _dot(s, w_ref[...].astype(BF16)) + b_ref[...]


def _ada(cond, ada_w, ada_b):
    tn = 1024
    n_out = N_MOD * D_MODEL
    return pl.pallas_call(
        _ada_body,
        out_shape=jax.ShapeDtypeStruct((DEPTH, COND_ROWS, n_out), F32),
        grid=(DEPTH, n_out // tn),
        in_specs=[pl.BlockSpec((COND_ROWS, D_MODEL), lambda i, n: (0, 0)),
                  pl.BlockSpec((None, D_MODEL, tn), lambda i, n: (i, 0, n)),
                  pl.BlockSpec((None, 1, tn), lambda i, n: (i, 0, n))],
        out_specs=pl.BlockSpec((None, COND_ROWS, tn), lambda i, n: (i, 0, n)),
        compiler_params=_params(("arbitrary", "arbitrary")),
        name="ada",
    )(cond, ada_w, ada_b.reshape(DEPTH, 1, n_out))


def _mm_body(*refs, n_x, n_w, swiglu):
    x_refs = refs[:n_x]
    w_refs = refs[n_x:n_x + n_w]
    o_ref = refs[n_x + n_w]
    wb_refs = refs[n_x + n_w + 1:]

    @pl.when(pl.program_id(1) == 0)
    def _():
        for w_ref, wb in zip(w_refs, wb_refs):
            wb[...] = w_ref[...].astype(BF16)

    if n_x == 1:
        x = x_refs[0][...]
    else:
        x = jnp.concatenate([r[...] for r in x_refs], axis=1)
    accs = [_dot(x, wb[...]) for wb in wb_refs]
    out = _silu(accs[0]) * accs[1] if swiglu else accs[0]
    o_ref[...] = out.astype(o_ref.dtype)


def _matmul(xs, ws, *, layer, n_cols, tn, tm, out_dtype, col0=0, swiglu=False, name):
    m = xs[0].shape[0]
    k = sum(x.shape[1] for x in xs)
    assert m % tm == 0 and n_cols % tn == 0 and col0 % tn == 0 and all(w.shape[1] == k for w in ws)
    cb0 = col0 // tn
    in_specs = [pl.BlockSpec((tm, x.shape[1]), lambda n, r: (r, 0)) for x in xs]
    in_specs += [pl.BlockSpec((None, k, tn), lambda n, r: (layer, 0, cb0 + n)) for _ in ws]
    return pl.pallas_call(
        functools.partial(_mm_body, n_x=len(xs), n_w=len(ws), swiglu=swiglu),
        out_shape=jax.ShapeDtypeStruct((m, n_cols), out_dtype),
        grid=(n_cols // tn, m // tm),
        in_specs=in_specs,
        out_specs=pl.BlockSpec((tm, tn), lambda n, r: (r, n)),
        scratch_shapes=[pltpu.VMEM((k, tn), BF16) for _ in ws],
        compiler_params=_params(("arbitrary", "arbitrary")),
        name=name,
    )(*xs, *ws)


_LN_TM = 256
_P_TILES = N_TOK_P // _LN_TM


def _modulate_body(xp_ref, xs_ref, sc_ref, sh_ref, x_ref, u_ref):
    def emit(x):
        x_ref[...] = x
        u_ref[...] = (x * (1.0 + sc_ref[...]) + sh_ref[...]).astype(BF16)

    @pl.when(pl.program_id(0) < _P_TILES)
    def _():
        emit(xp_ref[...])

    @pl.when(pl.program_id(0) >= _P_TILES)
    def _():
        emit(xs_ref[...])


def _modulate(x_prompt, x_sample, mod5, layer):
    tm = _LN_TM
    row = pl.BlockSpec((tm, D_MODEL), lambda t: (t, 0))
    return pl.pallas_call(
        _modulate_body,
        out_shape=[jax.ShapeDtypeStruct((N_TOK, D_MODEL), F32), jax.ShapeDtypeStruct((N_TOK, D_MODEL), BF16)],
        grid=(N_TOK // tm,),
        in_specs=[pl.BlockSpec((tm, D_MODEL), lambda t: (jnp.minimum(t, _P_TILES - 1), 0)),
                  pl.BlockSpec((tm, D_MODEL), lambda t: (jnp.maximum(t - _P_TILES, 0), 0)),
                  _mod_spec(layer, 1, tm), _mod_spec(layer, 0, tm)],
        out_specs=[row, row],
        compiler_params=_params(("arbitrary",)),
        name="modulate",
    )(x_prompt.reshape(N_TOK_P, D_MODEL), x_sample.reshape(N_TOK_S, D_MODEL), mod5, mod5)


def _ln_body(*refs, modulate, route, gated_pair):
    if gated_pair:
        x_ref, h_ref, h2_ref, pr_ref, g_ref, lg_ref, lb_ref = refs[:7]
        refs = refs[7:]
        pr = pr_ref[...]
        h = h_ref[...].astype(F32) * pr[:, 0:1] + h2_ref[...].astype(F32) * pr[:, 1:2]
    else:
        x_ref, h_ref, g_ref, lg_ref, lb_ref = refs[:5]
        refs = refs[5:]
        h = h_ref[...].astype(F32)
    v = ALPHA * x_ref[...] + g_ref[...] * h
    mu = jnp.mean(v, axis=-1, keepdims=True)
    d = v - mu
    var = jnp.mean(d * d, axis=-1, keepdims=True)
    xn = d * lax.rsqrt(var + EPS) * lg_ref[...] + lb_ref[...]
    if not modulate:
        yp_ref, ys_ref = refs

        @pl.when(pl.program_id(0) < _P_TILES)
        def _():
            yp_ref[...] = xn

        @pl.when(pl.program_id(0) >= _P_TILES)
        def _():
            ys_ref[...] = xn
        return
    sc_ref, sh_ref = refs[:2]
    refs = refs[2:]
    u = xn * (1.0 + sc_ref[...]) + sh_ref[...]
    if not route:
        xo_ref, uo_ref = refs
        xo_ref[...] = xn
        uo_ref[...] = u.astype(BF16)
        return
    r_ref, xo_ref, uo_ref, p_ref, i_ref = refs
    xo_ref[...] = xn
    uo_ref[...] = u.astype(BF16)
    r_hi, r_lo = _split2(r_ref[...])
    logits = _dot3(u, r_hi, r_lo)
    lane = lax.broadcasted_iota(jnp.int32, logits.shape, 1)
    l1 = jnp.where(lane < N_EXPERTS, logits, -jnp.inf)
    m1 = jnp.max(l1, axis=-1, keepdims=True)
    i1 = jnp.min(jnp.where(l1 == m1, lane, LANES), axis=-1, keepdims=True)
    l2 = jnp.where(lane == i1, -jnp.inf, l1)
    m2 = jnp.max(l2, axis=-1, keepdims=True)
    i2 = jnp.min(jnp.where(l2 == m2, lane, LANES), axis=-1, keepdims=True)
    e = jnp.exp(m2 - m1)
    p1 = 1.0 / (1.0 + e)
    p2 = e / (1.0 + e)
    p_ref[...] = jnp.where(lane == 0, p1, jnp.where(lane == 1, p2, 0.0))
    i_ref[...] = jnp.where(lane == 0, i1, jnp.where(lane == 1, i2, 0))


def _ln(x, h, mod5, ln_g, ln_b, *, layer, gate_chunk, mod_next=None, router=None, pair_probs=None):
    tm = _LN_TM
    row = pl.BlockSpec((tm, D_MODEL), lambda t: (t, 0))
    vec = pl.BlockSpec((None, 1, D_MODEL), lambda t: (layer, 0, 0))
    lane_row = pl.BlockSpec((tm, LANES), lambda t: (t, 0))
    if pair_probs is None:
        in_specs = [row, row]
        args = [x, h]
    else:
        in_specs = [row, row, row, lane_row]
        args = [x, h[0], h[1], pair_probs]
    in_specs += [_mod_spec(layer, gate_chunk, tm), vec, vec]
    args += [mod5, ln_g.reshape(DEPTH, 1, D_MODEL), ln_b.reshape(DEPTH, 1, D_MODEL)]
    if mod_next is None:
        out_shape = [jax.ShapeDtypeStruct((N_TOK_P, D_MODEL), F32), jax.ShapeDtypeStruct((N_TOK_S, D_MODEL), F32)]
        out_specs = [pl.BlockSpec((tm, D_MODEL), lambda t: (jnp.minimum(t, _P_TILES - 1), 0)),
                     pl.BlockSpec((tm, D_MODEL), lambda t: (jnp.maximum(t - _P_TILES, 0), 0))]
    else:
        nl, sc_chunk, sh_chunk = mod_next
        in_specs += [_mod_spec(nl, sc_chunk, tm), _mod_spec(nl, sh_chunk, tm)]
        args += [mod5, mod5]
        out_shape = [jax.ShapeDtypeStruct((N_TOK, D_MODEL), F32), jax.ShapeDtypeStruct((N_TOK, D_MODEL), BF16)]
        out_specs = [row, row]
    if router is not None:
        in_specs.append(pl.BlockSpec((D_MODEL, LANES), lambda t: (0, 0)))
        args.append(router)
        out_shape += [jax.ShapeDtypeStruct((N_TOK, LANES), F32), jax.ShapeDtypeStruct((N_TOK, LANES), jnp.int32)]
        out_specs += [lane_row, lane_row]
    return pl.pallas_call(
        functools.partial(_ln_body, modulate=mod_next is not None, route=router is not None,
                          gated_pair=pair_probs is not None),
        out_shape=out_shape,
        grid=(N_TOK // tm,),
        in_specs=in_specs,
        out_specs=out_specs,
        compiler_params=_params(("arbitrary",)),
        name="ln",
    )(*args)


def _rms(x, w):
    return x * lax.rsqrt(jnp.mean(x * x, axis=-1, keepdims=True) + EPS) * w


def _split_body(p_ref, qw_ref, kw_ref, qn_ref, ckv_ref, kr_ref):
    q0 = FOURIER_DIM
    k0 = q0 + Q_RANK
    r0 = k0 + KV_RANK
    qn_ref[...] = _rms(p_ref[:, q0:k0], qw_ref[...]).astype(BF16)
    ckv_ref[...] = _rms(p_ref[:, k0:r0], kw_ref[...])
    kr_ref[...] = p_ref[:, r0:]


def _split_proj(proj, q_norm_w, kv_norm_w, layer, tm=512):
    return pl.pallas_call(
        _split_body,
        out_shape=[jax.ShapeDtypeStruct((N_TOK, Q_RANK), BF16),
                   jax.ShapeDtypeStruct((N_TOK, KV_RANK), F32),
                   jax.ShapeDtypeStruct((N_TOK, ROPE_DIM), F32)],
        grid=(N_TOK // tm,),
        in_specs=[pl.BlockSpec((tm, MIX_IN), lambda t: (t, 0)),
                  pl.BlockSpec((None, 1, Q_RANK), lambda t: (layer, 0, 0)),
                  pl.BlockSpec((None, 1, KV_RANK), lambda t: (layer, 0, 0))],
        out_specs=[pl.BlockSpec((tm, Q_RANK), lambda t: (t, 0)),
                   pl.BlockSpec((tm, KV_RANK), lambda t: (t, 0)),
                   pl.BlockSpec((tm, ROPE_DIM), lambda t: (t, 0))],
        compiler_params=_params(("arbitrary",)),
        name="split_proj",
    )(proj, q_norm_w.reshape(N_EVEN, 1, Q_RANK), kv_norm_w.reshape(N_EVEN, 1, KV_RANK))


def _bf16_pair(a):
    hi = a.astype(np.float32).astype(jnp.bfloat16)
    lo = (a - hi.astype(np.float64)).astype(np.float32).astype(jnp.bfloat16)
    return np.stack([hi, lo])


@functools.lru_cache(maxsize=None)
def _dft_tables(length):
    k = np.arange(length)
    ang = 2.0 * np.pi * (np.outer(k, k) % length) / length
    scale = 1.0 / np.sqrt(length * FOURIER_GROUP_DIM)
    t_len = np.concatenate([np.cos(ang), -np.sin(ang)], axis=1) * scale
    c = np.arange(FOURIER_GROUP_DIM)
    angc = 2.0 * np.pi * (np.outer(c, c) % FOURIER_GROUP_DIM) / FOURIER_GROUP_DIM
    eye = np.eye(FOURIER_GROUPS)
    t_ch = np.concatenate([np.kron(eye, np.cos(angc)), np.kron(eye, np.sin(angc))], axis=1)
    return _bf16_pair(t_len), _bf16_pair(t_ch)


def _fourier_body(f_ref, tl_ref, tc_ref, *rest):
    o_ref = rest[-1]
    gh = _dot3(f_ref[...], tc_ref[0], tc_ref[1])
    stacked = jnp.concatenate([gh[:, :FOURIER_DIM], gh[:, FOURIER_DIM:]], axis=0)
    s_hi, s_lo = _split2(stacked)
    y = _dot(tl_ref[0], s_hi) + _dot(tl_ref[1], s_hi) + _dot(tl_ref[0], s_lo)
    o_ref[...] = y.astype(BF16)


def _fourier(proj, n_seq, length, row0, prev=None):
    t_len, t_ch = _dft_tables(length)
    blk0 = row0 // length
    in_specs = [pl.BlockSpec((length, FOURIER_DIM), lambda b: (blk0 + b, 0)),
                pl.BlockSpec((2, length, 2 * length), lambda b: (0, 0, 0)),
                pl.BlockSpec((2, FOURIER_DIM, 2 * FOURIER_DIM), lambda b: (0, 0, 0))]
    args = [proj, jnp.asarray(t_len), jnp.asarray(t_ch)]
    aliases = {}
    if prev is not None:
        in_specs.append(ANY_SPEC)
        args.append(prev)
        aliases = {3: 0}
    return pl.pallas_call(
        _fourier_body,
        out_shape=jax.ShapeDtypeStruct((N_TOK, FOURIER_DIM), BF16),
        grid=(n_seq,),
        in_specs=in_specs,
        out_specs=pl.BlockSpec((length, FOURIER_DIM), lambda b: (blk0 + b, 0)),
        input_output_aliases=aliases,
        compiler_params=_params(("arbitrary",)),
        name="fourier",
    )(*args)


def _rope_pair(x, cos, sin_signed):
    lane = lax.broadcasted_iota(jnp.int32, x.shape, 1)
    first_half = (lane & 31) < 16
    partner = jnp.where(first_half, pltpu.roll(x, LANES - 16, 1), pltpu.roll(x, 16, 1))
    return x * cos + partner * sin_signed


def _attn_body(*refs, rope, ctx):
    q_ref, kv_ref, kr_ref = refs[:3]
    refs = refs[3:]
    if rope:
        cq_ref, sq_ref, ck_ref, sk_ref = refs[:4]
        refs = refs[4:]
    if ctx:
        kvc_ref, krc_ref = refs[:2]
        refs = refs[2:]
    o_ref = refs[-1]

    kr = kr_ref[...]
    if rope:
        kr = _rope_pair(jnp.concatenate([kr, kr], axis=1), ck_ref[...], sk_ref[...])[:, :ROPE_DIM]
    kr = kr.astype(BF16)
    if ctx:
        krc = krc_ref[...].astype(BF16)
    rope0 = MLA_HEADS * NOPE_DIM
    for pr in range(MLA_HEADS // 2):
        qr = q_ref[:, rope0 + pr * LANES:rope0 + (pr + 1) * LANES]
        if rope:
            qr = _rope_pair(qr, cq_ref[...], sq_ref[...])
        qr = (qr * ATTN_SCALE).astype(BF16)
        for hh in range(2):
            h = 2 * pr + hh
            qn = (q_ref[:, h * NOPE_DIM:(h + 1) * NOPE_DIM] * ATTN_SCALE).astype(BF16)
            qrh = qr[:, hh * ROPE_DIM:(hh + 1) * ROPE_DIM]
            c0 = h * (NOPE_DIM + V_DIM)
            s = _dot_nt(qn, kv_ref[:, c0:c0 + NOPE_DIM]) + _dot_nt(qrh, kr)
            m = jnp.max(s, axis=-1, keepdims=True)
            if ctx:
                sc = _dot_nt(qn, kvc_ref[:, c0:c0 + NOPE_DIM]) + _dot_nt(qrh, krc)
                m = jnp.maximum(m, jnp.max(sc, axis=-1, keepdims=True))
            p = jnp.exp(s - m)
            den = jnp.sum(p, axis=-1, keepdims=True)
            acc = _dot(p.astype(BF16), kv_ref[:, c0 + NOPE_DIM:c0 + NOPE_DIM + V_DIM])
            if ctx:
                pc = jnp.exp(sc - m)
                den = den + jnp.sum(pc, axis=-1, keepdims=True)
                acc = acc + _dot(pc.astype(BF16), kvc_ref[:, c0 + NOPE_DIM:c0 + NOPE_DIM + V_DIM])
            o_ref[:, h * V_DIM:(h + 1) * V_DIM] = (acc / den).astype(BF16)


def _attention(q, kv, kr, *, n_seq, length, row0, tq, rope_tabs=None, ctx=None, prev=None):
    n_qt = length // tq
    qblk0 = row0 // tq
    kblk0 = row0 // length
    kv_w = MLA_HEADS * (NOPE_DIM + V_DIM)
    in_specs = [pl.BlockSpec((tq, MLA_HEADS * QK_DIM), lambda b, i: (qblk0 + b * n_qt + i, 0)),
                pl.BlockSpec((length, kv_w), lambda b, i: (kblk0 + b, 0)),
                pl.BlockSpec((length, ROPE_DIM), lambda b, i: (kblk0 + b, 0))]
    args = [q, kv, kr]
    if rope_tabs is not None:
        cos, sin = rope_tabs
        in_specs += [pl.BlockSpec((tq, LANES), lambda b, i: (i, 0)),
                     pl.BlockSpec((tq, LANES), lambda b, i: (i, 0)),
                     pl.BlockSpec((length, LANES), lambda b, i: (0, 0)),
                     pl.BlockSpec((length, LANES), lambda b, i: (0, 0))]
        args += [cos, sin, cos, sin]
    if ctx is not None:
        krc, cblk0 = ctx
        in_specs += [pl.BlockSpec((PAST_LEN, kv_w), lambda b, i: (cblk0 + b, 0)),
                     pl.BlockSpec((None, PAST_LEN, ROPE_DIM), lambda b, i: (b, 0, 0))]
        args += [kv, krc]
    aliases = {}
    if prev is not None:
        aliases = {len(args): 0}
        in_specs.append(ANY_SPEC)
        args.append(prev)
    return pl.pallas_call(
        functools.partial(_attn_body, rope=rope_tabs is not None, ctx=ctx is not None),
        out_shape=jax.ShapeDtypeStruct((N_TOK, MLA_HEADS * V_DIM), BF16),
        grid=(n_seq, n_qt),
        in_specs=in_specs,
        out_specs=pl.BlockSpec((tq, MLA_HEADS * V_DIM), lambda b, i: (qblk0 + b * n_qt + i, 0)),
        input_output_aliases=aliases,
        compiler_params=_params(("arbitrary", "arbitrary")),
        name="attention",
    )(*args)


def _rope_tables():
    rows = DEC_SEQ // GRID_W
    row = jnp.repeat(jnp.arange(rows, dtype=F32), GRID_W)
    col = jnp.tile(jnp.arange(GRID_W, dtype=F32), rows)
    inv = ROPE_THETA ** (-jnp.arange(ROPE_DIM // 4, dtype=F32) * 2.0 / (ROPE_DIM // 2))
    ang = jnp.stack([row[:, None] * inv, col[:, None] * inv], axis=1)
    cos, sin = jnp.cos(ang), jnp.sin(ang)
    cos64 = jnp.stack([cos, cos], axis=2).reshape(DEC_SEQ, ROPE_DIM)
    sin64 = jnp.stack([-sin, sin], axis=2).reshape(DEC_SEQ, ROPE_DIM)
    return jnp.tile(cos64, (1, 2)), jnp.tile(sin64, (1, 2))


_CONV_PAD = 8
_DIR_HEADS = 2 * HEADS_PER_GROUP


@functools.lru_cache(maxsize=None)
def _head_expand():
    e = np.zeros((_DIR_HEADS, 2 * GROUP_DIM), np.float32)
    for h in range(HEADS_PER_GROUP):
        e[h, h * SSM_HEAD_DIM:(h + 1) * SSM_HEAD_DIM] = 1.0
        e[HEADS_PER_GROUP + h, GROUP_DIM + h * SSM_HEAD_DIM:GROUP_DIM + (h + 1) * SSM_HEAD_DIM] = 1.0
    return e.astype(jnp.bfloat16)


def _scan_body(*refs, length, zero_init, emit_state, n_prev):
    (xr_ref, br_ref, cr_ref, wx_ref, wb_ref, wc_ref, bx_ref, bb_ref, bc_ref,
     dt_ref, bias_ref, acol_ref, dskip_ref, e_ref) = refs[:14]
    refs = refs[14:]
    if not zero_init:
        h0f_ref, h0b_ref = refs[:2]
        refs = refs[2:]
    refs = refs[n_prev:]
    y_ref = refs[0]
    refs = refs[1:]
    if emit_state:
        hf_ref, hb_ref = refs[:2]
        refs = refs[2:]
    padx, padb, padc, y_scr, bm_scr, cm_scr, xwf_scr, xwb_scr, decf_scr, decb_scr = refs
    q = SSM_CHUNK
    n_chunks = length // q
    hpg = HEADS_PER_GROUP
    hd = SSM_HEAD_DIM
    state_shape = (GROUP_DIM, SSM_STATE)

    for raw, pad in ((xr_ref, padx), (br_ref, padb), (cr_ref, padc)):
        zeros = jnp.zeros((_CONV_PAD, pad.shape[1]), F32)
        pad[0:_CONV_PAD, :] = zeros
        pad[_CONV_PAD + length:, :] = zeros
        pad[_CONV_PAD:_CONV_PAD + length, :] = raw[...].astype(F32)

    rr = lax.broadcasted_iota(jnp.int32, (q, q), 0)
    cc = lax.broadcasted_iota(jnp.int32, (q, q), 1)
    lower = cc <= rr
    upper = cc >= rr
    tri_t = upper.astype(BF16)
    pair_lane = lax.broadcasted_iota(jnp.int32, (q, 2 * hd), 1)
    fwd_rows = lax.broadcasted_iota(jnp.int32, (_DIR_HEADS, q), 0) < hpg
    pad_rows = jnp.zeros((q - _DIR_HEADS, q), F32)
    a_neg = -jnp.exp(acol_ref[...])
    bias = bias_ref[...]
    e16 = e_ref[...]

    def conv_silu(pad, w_ref, b_ref, r0):
        n_win = q + 2 * _CONV_PAD
        win = pad[r0:r0 + n_win, :]
        acc = b_ref[...]
        for k in range(SSM_CONV):
            off = _CONV_PAD + k - SSM_CONV // 2
            tap = win[off:off + q, :] if off == _CONV_PAD else pltpu.roll(win, n_win - off, 0)[0:q, :]
            acc = acc + w_ref[k:k + 1, :] * tap
        return _silu(acc)

    totals = []
    for c in range(n_chunks):
        r0 = c * q
        rows = slice(r0, r0 + q)
        x = conv_silu(padx, wx_ref, bx_ref, r0)
        bm = conv_silu(padb, wb_ref, bb_ref, r0).astype(BF16)
        cm = conv_silu(padc, wc_ref, bc_ref, r0).astype(BF16)
        bm_scr[rows, :] = bm
        cm_scr[rows, :] = cm
        t = (dt_ref[rows, :] + bias).T[:_DIR_HEADS, :]
        dt_t = jnp.maximum(t, 0.0) + jnp.log1p(jnp.exp(-jnp.abs(t)))
        a_t = dt_t * a_neg
        a_hi = a_t.astype(BF16)
        r1 = a_t - a_hi.astype(F32)
        a_mid = r1.astype(BF16)
        a_lo = (r1 - a_mid.astype(F32)).astype(BF16)
        cs_t = _dot(a_hi, tri_t) + _dot(a_mid, tri_t) + _dot(a_lo, tri_t)
        ecs_t = cs_t - a_t
        total = cs_t[:, q - 1:q]
        totals.append(jnp.exp(total))
        cs = jnp.concatenate([cs_t, pad_rows], axis=0).T
        ecs = jnp.concatenate([ecs_t, pad_rows], axis=0).T
        cb = _dot_nt(cm, bm)
        pieces = []
        for pr in range(hpg // 2):
            mats = []
            for h in (2 * pr, 2 * pr + 1):
                lf = jnp.exp(jnp.where(lower, cs[:, h:h + 1] - cs_t[h:h + 1, :], NEG_BIG))
                lb = jnp.exp(jnp.where(upper, ecs_t[hpg + h:hpg + h + 1, :] - ecs[:, hpg + h:hpg + h + 1], NEG_BIG))
                mats.append((cb * (lf * dt_t[h:h + 1, :] + lb * dt_t[hpg + h:hpg + h + 1, :])).astype(BF16))
            xp = x[:, pr * 2 * hd:(pr + 1) * 2 * hd]
            rhs = jnp.concatenate([jnp.where(pair_lane < hd, xp, 0.0), jnp.where(pair_lane >= hd, xp, 0.0)], axis=0)
            pieces.append(_dot(jnp.concatenate(mats, axis=1), rhs.astype(BF16)))
        y_scr[rows, :] = jnp.concatenate(pieces, axis=1) + dskip_ref[...] * x
        dec = _dot_tn(jnp.exp(jnp.where(fwd_rows, cs_t, total - ecs_t)).astype(BF16), e16)
        wgt = _dot_tn((jnp.exp(jnp.where(fwd_rows, total - cs_t, ecs_t)) * dt_t).astype(BF16), e16)
        decf_scr[rows, :] = dec[:, :GROUP_DIM].astype(BF16)
        decb_scr[rows, :] = dec[:, GROUP_DIM:].astype(BF16)
        xwf_scr[rows, :] = (x * wgt[:, :GROUP_DIM]).astype(BF16)
        xwb_scr[rows, :] = (x * wgt[:, GROUP_DIM:]).astype(BF16)

    def step(state, c, dec_scr, xw_scr, row0):
        rows = slice(c * q, (c + 1) * q)
        y_off = _dot_nt(cm_scr[rows, :], state.astype(BF16))
        y_scr[rows, :] = y_scr[rows, :] + y_off * dec_scr[rows, :].astype(F32)
        upd = _dot_tn(xw_scr[rows, :], bm_scr[rows, :])
        decay = totals[c]
        return jnp.concatenate(
            [state[h * hd:(h + 1) * hd, :] * decay[row0 + h:row0 + h + 1, :] + upd[h * hd:(h + 1) * hd, :]
             for h in range(hpg)], axis=0)

    if zero_init:
        hf = jnp.zeros(state_shape, F32)
        hb = jnp.zeros(state_shape, F32)
    else:
        hf = h0f_ref[...].reshape(state_shape)
        hb = h0b_ref[...].reshape(state_shape)
    for i in range(n_chunks):
        hf = step(hf, i, decf_scr, xwf_scr, 0)
        hb = step(hb, n_chunks - 1 - i, decb_scr, xwb_scr, hpg)
    y_ref[...] = y_scr[...].astype(BF16)
    if emit_state:
        hf_ref[...] = hf.reshape(hf_ref.shape)
        hb_ref[...] = hb.reshape(hb_ref.shape)


def _scan(xbc, dtg, conv_w, conv_b, bias_g, alog_col, dskip, layer, *, n_seq, length, row0, h0=None,
          y_prev=None, state_prev=None, emit_state):
    blk0 = row0 // length
    b0 = SSM_INNER // SSM_STATE
    c0 = b0 + SSM_GROUPS
    seq = lambda width, col: pl.BlockSpec((length, width), lambda b, g: (blk0 + b, col + g))
    cw = lambda width, col: pl.BlockSpec((None, SSM_CONV, width), lambda b, g: (layer, 0, col + g))
    cbias = lambda width, col: pl.BlockSpec((None, 1, width), lambda b, g: (layer, 0, col + g))
    grp = lambda width: pl.BlockSpec((None, 1, width), lambda b, g: (layer, 0, g))
    h0_spec = pl.BlockSpec((None, None, HEADS_PER_GROUP, SSM_HEAD_DIM, SSM_STATE), lambda b, g: (b, layer, g, 0, 0))
    in_specs = [seq(GROUP_DIM, 0), seq(SSM_STATE, b0), seq(SSM_STATE, c0),
                cw(GROUP_DIM, 0), cw(SSM_STATE, b0), cw(SSM_STATE, c0),
                cbias(GROUP_DIM, 0), cbias(SSM_STATE, b0), cbias(SSM_STATE, c0),
                seq(LANES, 0), grp(LANES),
                pl.BlockSpec((None, None, _DIR_HEADS, 1), lambda b, g: (layer, g, 0, 0)), grp(GROUP_DIM),
                pl.BlockSpec((_DIR_HEADS, 2 * GROUP_DIM), lambda b, g: (0, 0))]
    cb3 = conv_b.reshape(N_ODD, 1, SSM_CONV_DIM)
    args = [xbc, xbc, xbc, conv_w, conv_w, conv_w, cb3, cb3, cb3, dtg, bias_g, alog_col, dskip,
            jnp.asarray(_head_expand())]
    if h0 is not None:
        in_specs += [h0_spec, h0_spec]
        args += list(h0)
    out_shape = [jax.ShapeDtypeStruct((N_TOK, SSM_INNER), BF16)]
    out_specs = [seq(GROUP_DIM, 0)]
    if emit_state:
        st = jax.ShapeDtypeStruct((n_seq, N_ODD, SSM_HEADS, SSM_HEAD_DIM, SSM_STATE), F32)
        out_shape += [st, st]
        out_specs += [h0_spec, h0_spec]
    aliases = {}
    prevs = ([] if y_prev is None else [y_prev]) + ([] if state_prev is None else list(state_prev))
    for k, prev in enumerate(prevs):
        aliases[len(args)] = k if y_prev is not None else k + 1
        in_specs.append(ANY_SPEC)
        args.append(prev)
    pad_rows = length + 2 * _CONV_PAD
    scratch = [pltpu.VMEM((pad_rows, GROUP_DIM), F32), pltpu.VMEM((pad_rows, SSM_STATE), F32),
               pltpu.VMEM((pad_rows, SSM_STATE), F32), pltpu.VMEM((length, GROUP_DIM), F32),
               pltpu.VMEM((length, SSM_STATE), BF16), pltpu.VMEM((length, SSM_STATE), BF16),
               pltpu.VMEM((length, GROUP_DIM), BF16), pltpu.VMEM((length, GROUP_DIM), BF16),
               pltpu.VMEM((length, GROUP_DIM), BF16), pltpu.VMEM((length, GROUP_DIM), BF16)]
    return pl.pallas_call(
        functools.partial(_scan_body, length=length, zero_init=h0 is None, emit_state=emit_state,
                          n_prev=len(prevs)),
        out_shape=out_shape,
        grid=(n_seq, SSM_GROUPS),
        in_specs=in_specs,
        out_specs=out_specs,
        scratch_shapes=scratch,
        input_output_aliases=aliases,
        compiler_params=_params(("arbitrary", "arbitrary")),
        name="ssd_scan",
    )(*args)


def _gate_norm_body(y_ref, z_ref, w_ref, o_ref):
    g = y_ref[...].astype(F32) * _silu(z_ref[...].astype(F32))
    o_ref[...] = _rms(g, w_ref[...]).astype(BF16)


def _gate_norm(y, z, norm_w, layer, tm=256):
    return pl.pallas_call(
        _gate_norm_body,
        out_shape=jax.ShapeDtypeStruct((N_TOK, SSM_INNER), BF16),
        grid=(N_TOK // tm,),
        in_specs=[pl.BlockSpec((tm, SSM_INNER), lambda t: (t, 0)),
                  pl.BlockSpec((tm, SSM_INNER), lambda t: (t, 0)),
                  pl.BlockSpec((None, 1, SSM_INNER), lambda t: (layer, 0, 0))],
        out_specs=pl.BlockSpec((tm, SSM_INNER), lambda t: (t, 0)),
        compiler_params=_params(("arbitrary",)),
        name="gate_norm",
    )(y, z, norm_w.reshape(N_ODD, 1, SSM_INNER))


def _moe_up_body(e_ref, n_ref, m_ref, first_ref, valid_ref, x_ref, wg_ref, wu_ref, o_ref, gb, ub):
    s = pl.program_id(0)

    @pl.when(first_ref[s] == 1)
    def _():
        gb[...] = wg_ref[...].astype(BF16)
        ub[...] = wu_ref[...].astype(BF16)

    @pl.when(valid_ref[s] == 1)
    def _():
        x = x_ref[...]
        o_ref[...] = (_silu(_dot(x, gb[...])) * _dot(x, ub[...])).astype(BF16)


def _moe_down_body(e_ref, n_ref, m_ref, first_ref, valid_ref, h_ref, w_ref, o_ref, wb):
    s = pl.program_id(0)

    @pl.when(first_ref[s] == 1)
    def _():
        wb[...] = w_ref[...].astype(BF16)

    @pl.when(valid_ref[s] == 1)
    def _():
        o_ref[...] = _dot(h_ref[...], wb[...]).astype(BF16)


def _moe_tables(tiles, n_col_tiles):
    n_steps = MOE_TILES * n_col_tiles
    tile_start = jnp.cumsum(tiles) - tiles
    step_end = jnp.cumsum(tiles * n_col_tiles)
    total = step_end[-1]
    s = jnp.minimum(jnp.arange(n_steps, dtype=jnp.int32), total - 1)
    e = jnp.sum((s[:, None] >= step_end[None, :]).astype(jnp.int32), axis=1)
    local = s - (step_end - tiles * n_col_tiles)[e]
    te = jnp.maximum(tiles[e], 1)
    n = local // te
    k = local - n * te
    m = tile_start[e] + k
    valid = (jnp.arange(n_steps) < total).astype(jnp.int32)
    first = ((k == 0) & (valid == 1)).astype(jnp.int32)
    return e.astype(jnp.int32), n.astype(jnp.int32), m.astype(jnp.int32), first, valid


def _moe(u, idx, w_gate, w_up, w_down, layer):
    tm = MOE_TM
    flat_e = idx.reshape(-1)
    onehot = (flat_e[:, None] == jnp.arange(N_EXPERTS)[None, :]).astype(jnp.int32)
    rank = jnp.sum((jnp.cumsum(onehot, axis=0) - onehot) * onehot, axis=1)
    count = jnp.sum(onehot, axis=0)
    tiles = (count + tm - 1) // tm
    row_start = (jnp.cumsum(tiles) - tiles) * tm
    dest = row_start[flat_e] + rank
    token = jnp.arange(N_TOK * TOP_K, dtype=jnp.int32) // TOP_K
    src = jnp.zeros((MOE_ROWS,), jnp.int32).at[dest].set(token, mode="promise_in_bounds", unique_indices=True)
    x_sorted = u.at[src].get(mode="promise_in_bounds")

    tn = 1024
    n_up = EXPERT_DIM // tn
    tabs = _moe_tables(tiles, n_up)
    w_spec = pl.BlockSpec((None, None, D_MODEL, tn), lambda s, e, n, m, f, v: (layer, e[s], 0, n[s]))
    hmid = pl.pallas_call(
        _moe_up_body,
        out_shape=jax.ShapeDtypeStruct((MOE_ROWS, EXPERT_DIM), BF16),
        grid_spec=pltpu.PrefetchScalarGridSpec(
            num_scalar_prefetch=5,
            grid=(MOE_TILES * n_up,),
            in_specs=[pl.BlockSpec((tm, D_MODEL), lambda s, e, n, m, f, v: (m[s], 0)), w_spec, w_spec],
            out_specs=pl.BlockSpec((tm, tn), lambda s, e, n, m, f, v: (m[s], n[s])),
            scratch_shapes=[pltpu.VMEM((D_MODEL, tn), BF16), pltpu.VMEM((D_MODEL, tn), BF16)]),
        compiler_params=_params(("arbitrary",)),
        name="moe_up",
    )(*tabs, x_sorted, w_gate, w_up)

    tn = 1024
    n_dn = D_MODEL // tn
    tabs = _moe_tables(tiles, n_dn)
    y_sorted = pl.pallas_call(
        _moe_down_body,
        out_shape=jax.ShapeDtypeStruct((MOE_ROWS, D_MODEL), BF16),
        grid_spec=pltpu.PrefetchScalarGridSpec(
            num_scalar_prefetch=5,
            grid=(MOE_TILES * n_dn,),
            in_specs=[pl.BlockSpec((tm, EXPERT_DIM), lambda s, e, n, m, f, v: (m[s], 0)),
                      pl.BlockSpec((None, None, EXPERT_DIM, tn), lambda s, e, n, m, f, v: (layer, e[s], 0, n[s]))],
            out_specs=pl.BlockSpec((tm, tn), lambda s, e, n, m, f, v: (m[s], n[s])),
            scratch_shapes=[pltpu.VMEM((EXPERT_DIM, tn), BF16)]),
        compiler_params=_params(("arbitrary",), vmem_limit=56 * 1024 * 1024),
        name="moe_down",
    )(*tabs, hmid, w_down)

    dest2 = dest.reshape(N_TOK, TOP_K)
    return (y_sorted.at[dest2[:, 0]].get(mode="promise_in_bounds"),
            y_sorted.at[dest2[:, 1]].get(mode="promise_in_bounds"))


def _q_perm():
    h = np.arange(MLA_HEADS)[:, None]
    nope = (h * QK_DIM + np.arange(NOPE_DIM)[None, :]).reshape(-1)
    rope = (h * QK_DIM + NOPE_DIM + np.arange(ROPE_DIM)[None, :]).reshape(-1)
    return np.concatenate([nope, rope])


def _even_mixer(u, j, p, cache_ckv, cache_krope, rope_tabs):
    proj = _matmul([u], [p['mix_w_in']], layer=j, n_cols=MIX_IN, tn=MIX_IN, tm=512, out_dtype=F32, name="mix_in")
    qn, ckv, kr = _split_proj(proj, p['q_norm_w'], p['kv_norm_w'], j)
    f = _fourier(proj, BATCH, SEQ, 0)
    f = _fourier(proj, DEC_BATCH, DEC_SEQ, N_TOK_P, prev=f)
    q = _matmul([qn], [p['w_q_perm']], layer=j, n_cols=MLA_HEADS * QK_DIM, tn=MLA_HEADS * QK_DIM, tm=1024,
                out_dtype=F32, name="q_proj")
    ckv_all = jnp.concatenate([ckv, cache_ckv[:, j].reshape(DEC_BATCH * PAST_LEN, KV_RANK)], axis=0).astype(BF16)
    kv = _matmul([ckv_all], [p['w_kv_b']], layer=j, n_cols=MLA_HEADS * (NOPE_DIM + V_DIM),
                 tn=MLA_HEADS * (NOPE_DIM + V_DIM), tm=1024, out_dtype=BF16, name="kv_proj")
    o = _attention(q, kv, kr, n_seq=BATCH, length=SEQ, row0=0, tq=SEQ)
    o = _attention(q, kv, kr, n_seq=DEC_BATCH, length=DEC_SEQ, row0=N_TOK_P, tq=256, rope_tabs=rope_tabs,
                   ctx=(cache_krope[:, j], N_TOK // PAST_LEN), prev=o)
    h = _matmul([f, o], [p['mix_w_out']], layer=j, n_cols=D_MODEL, tn=1024, tm=1024, out_dtype=BF16, name="mix_out")
    return h, ckv, kr


def _group_dt_columns(w):
    lead = w.shape[:-1]
    w = w.reshape(*lead, 2, SSM_GROUPS, HEADS_PER_GROUP)
    w = jnp.moveaxis(w, -3, -2).reshape(*lead, SSM_GROUPS, 2 * HEADS_PER_GROUP)
    pad = [(0, 0)] * (w.ndim - 1) + [(0, LANES - 2 * HEADS_PER_GROUP)]
    return jnp.pad(w, pad).reshape(*lead, SSM_GROUPS * LANES)


def _group_head_rows(w):
    w = w.reshape(N_ODD, 2, SSM_GROUPS, HEADS_PER_GROUP)
    return jnp.moveaxis(w, 1, 2).reshape(N_ODD, SSM_GROUPS, _DIR_HEADS, 1)


def _odd_mixer(u, j, p, state_f, state_b, state_prev):
    z = _matmul([u], [p['ssm_w_in']], layer=j, n_cols=SSM_INNER, tn=1024, tm=1024, out_dtype=BF16, name="ssm_in_z")
    xbc = _matmul([u], [p['ssm_w_in']], layer=j, n_cols=SSM_CONV_DIM, col0=SSM_INNER, tn=1024, tm=1024,
                  out_dtype=BF16, name="ssm_in_xbc")
    dtg = _matmul([u], [p['ssm_w_dtg']], layer=j, n_cols=SSM_GROUPS * LANES, tn=512, tm=1024, out_dtype=F32,
                  name="ssm_dt")
    common = (xbc, dtg, p['ssm_conv_w'], p['ssm_conv_b'], p['ssm_bias_g'], p['ssm_alog_col'], p['ssm_dskip'], j)
    y, hf, hb = _scan(*common, n_seq=BATCH, length=SEQ, row0=0, state_prev=state_prev, emit_state=True)
    (y,) = _scan(*common, n_seq=DEC_BATCH, length=DEC_SEQ, row0=N_TOK_P, h0=(state_f, state_b), y_prev=y,
                 emit_state=False)
    yn = _gate_norm(y, z, p['ssm_norm_w'], j)
    h = _matmul([yn], [p['ssm_w_out']], layer=j, n_cols=D_MODEL, tn=512, tm=1024, out_dtype=BF16, name="ssm_out")
    return h, hf, hb


def kernel(x_prompt, x_sample, cache_ckv, cache_krope, state_ssm_fwd, state_ssm_bwd, c, c_ctx, ada_w, ada_b, ln1_g, ln1_b, ln2_g, ln2_b, mix_w_in, q_norm_w, w_q_b, kv_norm_w, w_kv_b, mix_w_out, ffn_w_gate, ffn_w_up, ffn_w_down, ssm_w_in, ssm_conv_w, ssm_conv_b, ssm_dt_bias, ssm_a_log, ssm_d, ssm_norm_w, ssm_w_out, moe_router, moe_w_gate, moe_w_up, moe_w_down):
    p = dict(mix_w_in=mix_w_in, q_norm_w=q_norm_w, w_q_perm=w_q_b[:, :, _q_perm()], kv_norm_w=kv_norm_w,
             w_kv_b=w_kv_b, mix_w_out=mix_w_out, ssm_w_in=ssm_w_in, ssm_conv_w=ssm_conv_w, ssm_conv_b=ssm_conv_b,
             ssm_w_dtg=_group_dt_columns(ssm_w_in[:, :, SSM_INNER + SSM_CONV_DIM:]),
             ssm_bias_g=_group_dt_columns(ssm_dt_bias.reshape(N_ODD, 1, 2 * SSM_HEADS)),
             ssm_alog_col=_group_head_rows(ssm_a_log),
             ssm_dskip=jnp.repeat(ssm_d, SSM_HEAD_DIM, axis=1).reshape(N_ODD, 1, SSM_INNER),
             ssm_norm_w=ssm_norm_w, ssm_w_out=ssm_w_out)
    cond = jnp.concatenate([c_ctx[None, :], c, jnp.zeros((COND_ROWS - N_COND, D_MODEL), F32)], axis=0)
    mod = _ada(cond, ada_w, ada_b)
    mod5 = mod[:, :N_COND].reshape(DEPTH, N_COND, N_MOD, 1, D_MODEL)
    rope_tabs = _rope_tables()

    x, u = _modulate(x_prompt, x_sample, mod5, 0)
    ckvs, krs = [], []
    states = None
    for i in range(DEPTH):
        j = i // 2
        if i % 2 == 0:
            h, ckv, kr = _even_mixer(u, j, p, cache_ckv, cache_krope, rope_tabs)
            ckvs.append(ckv[:N_TOK_P].reshape(BATCH, SEQ, KV_RANK))
            krs.append(kr[:N_TOK_P].reshape(BATCH, SEQ, ROPE_DIM))
            x, u = _ln(x, h, mod5, ln1_g, ln1_b, layer=i, gate_chunk=2, mod_next=(i, 4, 3))
            hmid = _matmul([u], [ffn_w_gate, ffn_w_up], layer=j, n_cols=FFN_DIM, tn=512, tm=1024, out_dtype=BF16,
                           swiglu=True, name="ffn_up")
            f = _matmul([hmid], [ffn_w_down], layer=j, n_cols=D_MODEL, tn=512, tm=512, out_dtype=BF16,
                        name="ffn_down")
            pair_probs = None
        else:
            h, hf, hb = _odd_mixer(u, j, p, state_ssm_fwd, state_ssm_bwd, states)
            states = (hf, hb)
            router = jnp.pad(moe_router[j], ((0, 0), (0, LANES - N_EXPERTS)))
            x, u, pair_probs, idx = _ln(x, h, mod5, ln1_g, ln1_b, layer=i, gate_chunk=2, mod_next=(i, 4, 3),
                                        router=router)
            f = _moe(u, idx[:, :TOP_K], moe_w_gate, moe_w_up, moe_w_down, j)
        if i + 1 < DEPTH:
            x, u = _ln(x, f, mod5, ln2_g, ln2_b, layer=i, gate_chunk=5, mod_next=(i + 1, 1, 0),
                       pair_probs=pair_probs)
        else:
            y_prompt, y_sample = _ln(x, f, mod5, ln2_g, ln2_b, layer=i, gate_chunk=5, pair_probs=pair_probs)
    return (y_prompt.reshape(BATCH, SEQ, D_MODEL), y_sample.reshape(DEC_BATCH, DEC_SEQ, D_MODEL),
            jnp.stack(ckvs, axis=1), jnp.stack(krs, axis=1), states[0], states[1])
```

```python
import functools

import numpy as np
import jax
import jax.numpy as jnp
from jax import lax
from jax.experimental import pallas as pl
from jax.experimental.pallas import tpu as pltpu

D_MODEL = 2048
BATCH = 16
SEQ = 256
DEPTH = 4
DEC_BATCH = 2
DEC_SEQ = 1024
PAST_LEN = 512
GRID_W = 64
N_EVEN = (DEPTH + 1) // 2
N_ODD = DEPTH // 2
FOURIER_GROUPS = 4
FOURIER_GROUP_DIM = 128
FOURIER_DIM = FOURIER_GROUPS * FOURIER_GROUP_DIM
MLA_HEADS = 12
Q_RANK = 768
KV_RANK = 256
NOPE_DIM = 128
ROPE_DIM = 64
V_DIM = 128
QK_DIM = NOPE_DIM + ROPE_DIM
ATTN_SCALE = QK_DIM ** -0.5
ROPE_THETA = 10000.0
MIX_IN = FOURIER_DIM + Q_RANK + KV_RANK + ROPE_DIM
SSM_INNER = 2 * D_MODEL
SSM_HEAD_DIM = 64
SSM_HEADS = SSM_INNER // SSM_HEAD_DIM
SSM_GROUPS = 8
SSM_STATE = 128
SSM_CONV = 5
SSM_CHUNK = 128
SSM_CONV_DIM = SSM_INNER + 2 * SSM_GROUPS * SSM_STATE
FFN_DIM = 5632
N_EXPERTS = 8
TOP_K = 2
EXPERT_DIM = 4096
ALPHA = (2 * DEPTH) ** 0.25
EPS = 1e-5

F32 = jnp.float32
BF16 = jnp.bfloat16

N_TOK_P = BATCH * SEQ
N_TOK_S = DEC_BATCH * DEC_SEQ
N_TOK = N_TOK_P + N_TOK_S
N_MOD = 6
N_COND = 1 + DEC_BATCH
COND_ROWS = 8
HEADS_PER_GROUP = SSM_HEADS // SSM_GROUPS
GROUP_DIM = HEADS_PER_GROUP * SSM_HEAD_DIM
LANES = 128
V7X_VMEM_LIMIT = 52 * 1024 * 1024
NEG_BIG = -1e30

MOE_TM = 512
MOE_TILES = (N_TOK * TOP_K) // MOE_TM + N_EXPERTS
MOE_ROWS = MOE_TILES * MOE_TM

ANY_SPEC = pl.BlockSpec(memory_space=pl.ANY)


def _params(sem, vmem_limit=V7X_VMEM_LIMIT):
    return pltpu.CompilerParams(dimension_semantics=sem, vmem_limit_bytes=vmem_limit)


def _silu(x):
    return x * jax.nn.sigmoid(x)


def _split2(x):
    hi = x.astype(BF16)
    lo = (x - hi.astype(F32)).astype(BF16)
    return hi, lo


def _dot(a, b):
    return jnp.dot(a, b, preferred_element_type=F32)


def _dot_nt(a, b):
    return lax.dot_general(a, b, (((1,), (1,)), ((), ())), preferred_element_type=F32)


def _dot_tn(a, b):
    return lax.dot_general(a, b, (((0,), (0,)), ((), ())), preferred_element_type=F32)


def _dot3(a, b_hi, b_lo):
    a_hi, a_lo = _split2(a)
    return _dot(a_hi, b_hi) + _dot(a_lo, b_hi) + _dot(a_hi, b_lo)


def _cond_of_tile(t, tm):
    p_tiles = N_TOK_P // tm
    per_seq = DEC_SEQ // tm
    return jnp.where(t < p_tiles, 0, 1 + (t - p_tiles) // per_seq)


def _mod_spec(layer, chunk, tm):
    return pl.BlockSpec((None, None, None, 1, D_MODEL),
                        lambda t: (layer, _cond_of_tile(t, tm), chunk, 0, 0))


def _ada_body(c_ref, w_ref, b_ref, o_ref):
    s = _silu(c_ref[...]).astype(BF16)
    o_ref[...] = _dot(s, w_ref[...].astype(BF16)) + b_ref[...]


def _ada(cond, ada_w, ada_b):
    tn = 1024
    n_out = N_MOD * D_MODEL
    return pl.pallas_call(
        _ada_body,
        out_shape=jax.ShapeDtypeStruct((DEPTH, COND_ROWS, n_out), F32),
        grid=(DEPTH, n_out // tn),
        in_specs=[pl.BlockSpec((COND_ROWS, D_MODEL), lambda i, n: (0, 0)),
                  pl.BlockSpec((None, D_MODEL, tn), lambda i, n: (i, 0, n)),
                  pl.BlockSpec((None, 1, tn), lambda i, n: (i, 0, n))],
        out_specs=pl.BlockSpec((None, COND_ROWS, tn), lambda i, n: (i, 0, n)),
        compiler_params=_params(("arbitrary", "arbitrary")),
        name="ada",
    )(cond, ada_w, ada_b.reshape(DEPTH, 1, n_out))


def _mm_body(*refs, n_x, n_w, swiglu):
    x_refs = refs[:n_x]
    w_refs = refs[n_x:n_x + n_w]
    o_ref = refs[n_x + n_w]
    wb_refs = refs[n_x + n_w + 1:]

    @pl.when(pl.program_id(1) == 0)
    def _():
        for w_ref, wb in zip(w_refs, wb_refs):
            wb[...] = w_ref[...].astype(BF16)

    if n_x == 1:
        x = x_refs[0][...]
    else:
        x = jnp.concatenate([r[...] for r in x_refs], axis=1)
    accs = [_dot(x, wb[...]) for wb in wb_refs]
    out = _silu(accs[0]) * accs[1] if swiglu else accs[0]
    o_ref[...] = out.astype(o_ref.dtype)


def _matmul(xs, ws, *, layer, n_cols, tn, tm, out_dtype, col0=0, swiglu=False, name):
    m = xs[0].shape[0]
    k = sum(x.shape[1] for x in xs)
    assert m % tm == 0 and n_cols % tn == 0 and col0 % tn == 0 and all(w.shape[1] == k for w in ws)
    cb0 = col0 // tn
    in_specs = [pl.BlockSpec((tm, x.shape[1]), lambda n, r: (r, 0)) for x in xs]
    in_specs += [pl.BlockSpec((None, k, tn), lambda n, r: (layer, 0, cb0 + n)) for _ in ws]
    return pl.pallas_call(
        functools.partial(_mm_body, n_x=len(xs), n_w=len(ws), swiglu=swiglu),
        out_shape=jax.ShapeDtypeStruct((m, n_cols), out_dtype),
        grid=(n_cols // tn, m // tm),
        in_specs=in_specs,
        out_specs=pl.BlockSpec((tm, tn), lambda n, r: (r, n)),
        scratch_shapes=[pltpu.VMEM((k, tn), BF16) for _ in ws],
        compiler_params=_params(("arbitrary", "arbitrary")),
        name=name,
    )(*xs, *ws)


_LN_TM = 256
_P_TILES = N_TOK_P // _LN_TM


def _modulate_body(xp_ref, xs_ref, sc_ref, sh_ref, x_ref, u_ref):
    def emit(x):
        x_ref[...] = x
        u_ref[...] = (x * (1.0 + sc_ref[...]) + sh_ref[...]).astype(BF16)

    @pl.when(pl.program_id(0) < _P_TILES)
    def _():
        emit(xp_ref[...])

    @pl.when(pl.program_id(0) >= _P_TILES)
    def _():
        emit(xs_ref[...])


def _modulate(x_prompt, x_sample, mod5, layer):
    tm = _LN_TM
    row = pl.BlockSpec((tm, D_MODEL), lambda t: (t, 0))
    return pl.pallas_call(
        _modulate_body,
        out_shape=[jax.ShapeDtypeStruct((N_TOK, D_MODEL), F32), jax.ShapeDtypeStruct((N_TOK, D_MODEL), BF16)],
        grid=(N_TOK // tm,),
        in_specs=[pl.BlockSpec((tm, D_MODEL), lambda t: (jnp.minimum(t, _P_TILES - 1), 0)),
                  pl.BlockSpec((tm, D_MODEL), lambda t: (jnp.maximum(t - _P_TILES, 0), 0)),
                  _mod_spec(layer, 1, tm), _mod_spec(layer, 0, tm)],
        out_specs=[row, row],
        compiler_params=_params(("arbitrary",)),
        name="modulate",
    )(x_prompt.reshape(N_TOK_P, D_MODEL), x_sample.reshape(N_TOK_S, D_MODEL), mod5, mod5)


def _ln_body(*refs, modulate, route, gated_pair):
    if gated_pair:
        x_ref, h_ref, h2_ref, pr_ref, g_ref, lg_ref, lb_ref = refs[:7]
        refs = refs[7:]
        pr = pr_ref[...]
        h = h_ref[...].astype(F32) * pr[:, 0:1] + h2_ref[...].astype(F32) * pr[:, 1:2]
    else:
        x_ref, h_ref, g_ref, lg_ref, lb_ref = refs[:5]
        refs = refs[5:]
        h = h_ref[...].astype(F32)
    v = ALPHA * x_ref[...] + g_ref[...] * h
    mu = jnp.mean(v, axis=-1, keepdims=True)
    d = v - mu
    var = jnp.mean(d * d, axis=-1, keepdims=True)
    xn = d * lax.rsqrt(var + EPS) * lg_ref[...] + lb_ref[...]
    if not modulate:
        yp_ref, ys_ref = refs

        @pl.when(pl.program_id(0) < _P_TILES)
        def _():
            yp_ref[...] = xn

        @pl.when(pl.program_id(0) >= _P_TILES)
        def _():
            ys_ref[...] = xn
        return
    sc_ref, sh_ref = refs[:2]
    refs = refs[2:]
    u = xn * (1.0 + sc_ref[...]) + sh_ref[...]
    if not route:
        xo_ref, uo_ref = refs
        xo_ref[...] = xn
        uo_ref[...] = u.astype(BF16)
        return
    r_ref, xo_ref, uo_ref, p_ref, i_ref = refs
    xo_ref[...] = xn
    uo_ref[...] = u.astype(BF16)
    r_hi, r_lo = _split2(r_ref[...])
    logits = _dot3(u, r_hi, r_lo)
    lane = lax.broadcasted_iota(jnp.int32, logits.shape, 1)
    l1 = jnp.where(lane < N_EXPERTS, logits, -jnp.inf)
    m1 = jnp.max(l1, axis=-1, keepdims=True)
    i1 = jnp.min(jnp.where(l1 == m1, lane, LANES), axis=-1, keepdims=True)
    l2 = jnp.where(lane == i1, -jnp.inf, l1)
    m2 = jnp.max(l2, axis=-1, keepdims=True)
    i2 = jnp.min(jnp.where(l2 == m2, lane, LANES), axis=-1, keepdims=True)
    e = jnp.exp(m2 - m1)
    p1 = 1.0 / (1.0 + e)
    p2 = e / (1.0 + e)
    p_ref[...] = jnp.where(lane == 0, p1, jnp.where(lane == 1, p2, 0.0))
    i_ref[...] = jnp.where(lane == 0, i1, jnp.where(lane == 1, i2, 0))


def _ln(x, h, mod5, ln_g, ln_b, *, layer, gate_chunk, mod_next=None, router=None, pair_probs=None):
    tm = _LN_TM
    row = pl.BlockSpec((tm, D_MODEL), lambda t: (t, 0))
    vec = pl.BlockSpec((None, 1, D_MODEL), lambda t: (layer, 0, 0))
    lane_row = pl.BlockSpec((tm, LANES), lambda t: (t, 0))
    if pair_probs is None:
        in_specs = [row, row]
        args = [x, h]
    else:
        in_specs = [row, row, row, lane_row]
        args = [x, h[0], h[1], pair_probs]
    in_specs += [_mod_spec(layer, gate_chunk, tm), vec, vec]
    args += [mod5, ln_g.reshape(DEPTH, 1, D_MODEL), ln_b.reshape(DEPTH, 1, D_MODEL)]
    if mod_next is None:
        out_shape = [jax.ShapeDtypeStruct((N_TOK_P, D_MODEL), F32), jax.ShapeDtypeStruct((N_TOK_S, D_MODEL), F32)]
        out_specs = [pl.BlockSpec((tm, D_MODEL), lambda t: (jnp.minimum(t, _P_TILES - 1), 0)),
                     pl.BlockSpec((tm, D_MODEL), lambda t: (jnp.maximum(t - _P_TILES, 0), 0))]
    else:
        nl, sc_chunk, sh_chunk = mod_next
        in_specs += [_mod_spec(nl, sc_chunk, tm), _mod_spec(nl, sh_chunk, tm)]
        args += [mod5, mod5]
        out_shape = [jax.ShapeDtypeStruct((N_TOK, D_MODEL), F32), jax.ShapeDtypeStruct((N_TOK, D_MODEL), BF16)]
        out_specs = [row, row]
    if router is not None:
        in_specs.append(pl.BlockSpec((D_MODEL, LANES), lambda t: (0, 0)))
        args.append(router)
        out_shape += [jax.ShapeDtypeStruct((N_TOK, LANES), F32), jax.ShapeDtypeStruct((N_TOK, LANES), jnp.int32)]
        out_specs += [lane_row, lane_row]
    return pl.pallas_call(
        functools.partial(_ln_body, modulate=mod_next is not None, route=router is not None,
                          gated_pair=pair_probs is not None),
        out_shape=out_shape,
        grid=(N_TOK // tm,),
        in_specs=in_specs,
        out_specs=out_specs,
        compiler_params=_params(("arbitrary",)),
        name="ln",
    )(*args)


def _rms(x, w):
    return x * lax.rsqrt(jnp.mean(x * x, axis=-1, keepdims=True) + EPS) * w


def _split_body(p_ref, qw_ref, kw_ref, qn_ref, ckv_ref, kr_ref):
    q0 = FOURIER_DIM
    k0 = q0 + Q_RANK
    r0 = k0 + KV_RANK
    qn_ref[...] = _rms(p_ref[:, q0:k0], qw_ref[...]).astype(BF16)
    ckv_ref[...] = _rms(p_ref[:, k0:r0], kw_ref[...])
    kr_ref[...] = p_ref[:, r0:]


def _split_proj(proj, q_norm_w, kv_norm_w, layer, tm=512):
    return pl.pallas_call(
        _split_body,
        out_shape=[jax.ShapeDtypeStruct((N_TOK, Q_RANK), BF16),
                   jax.ShapeDtypeStruct((N_TOK, KV_RANK), F32),
                   jax.ShapeDtypeStruct((N_TOK, ROPE_DIM), F32)],
        grid=(N_TOK // tm,),
        in_specs=[pl.BlockSpec((tm, MIX_IN), lambda t: (t, 0)),
                  pl.BlockSpec((None, 1, Q_RANK), lambda t: (layer, 0, 0)),
                  pl.BlockSpec((None, 1, KV_RANK), lambda t: (layer, 0, 0))],
        out_specs=[pl.BlockSpec((tm, Q_RANK), lambda t: (t, 0)),
                   pl.BlockSpec((tm, KV_RANK), lambda t: (t, 0)),
                   pl.BlockSpec((tm, ROPE_DIM), lambda t: (t, 0))],
        compiler_params=_params(("arbitrary",)),
        name="split_proj",
    )(proj, q_norm_w.reshape(N_EVEN, 1, Q_RANK), kv_norm_w.reshape(N_EVEN, 1, KV_RANK))


def _bf16_pair(a):
    hi = a.astype(np.float32).astype(jnp.bfloat16)
    lo = (a - hi.astype(np.float64)).astype(np.float32).astype(jnp.bfloat16)
    return np.stack([hi, lo])


@functools.lru_cache(maxsize=None)
def _dft_tables(length):
    k = np.arange(length)
    ang = 2.0 * np.pi * (np.outer(k, k) % length) / length
    scale = 1.0 / np.sqrt(length * FOURIER_GROUP_DIM)
    t_len = np.concatenate([np.cos(ang), -np.sin(ang)], axis=1) * scale
    c = np.arange(FOURIER_GROUP_DIM)
    angc = 2.0 * np.pi * (np.outer(c, c) % FOURIER_GROUP_DIM) / FOURIER_GROUP_DIM
    eye = np.eye(FOURIER_GROUPS)
    t_ch = np.concatenate([np.kron(eye, np.cos(angc)), np.kron(eye, np.sin(angc))], axis=1)
    return _bf16_pair(t_len), _bf16_pair(t_ch)


def _fourier_body(f_ref, tl_ref, tc_ref, *rest):
    o_ref = rest[-1]
    gh = _dot3(f_ref[...], tc_ref[0], tc_ref[1])
    stacked = jnp.concatenate([gh[:, :FOURIER_DIM], gh[:, FOURIER_DIM:]], axis=0)
    s_hi, s_lo = _split2(stacked)
    y = _dot(tl_ref[0], s_hi) + _dot(tl_ref[1], s_hi) + _dot(tl_ref[0], s_lo)
    o_ref[...] = y.astype(BF16)


def _fourier(proj, n_seq, length, row0, prev=None):
    t_len, t_ch = _dft_tables(length)
    blk0 = row0 // length
    in_specs = [pl.BlockSpec((length, FOURIER_DIM), lambda b: (blk0 + b, 0)),
                pl.BlockSpec((2, length, 2 * length), lambda b: (0, 0, 0)),
                pl.BlockSpec((2, FOURIER_DIM, 2 * FOURIER_DIM), lambda b: (0, 0, 0))]
    args = [proj, jnp.asarray(t_len), jnp.asarray(t_ch)]
    aliases = {}
    if prev is not None:
        in_specs.append(ANY_SPEC)
        args.append(prev)
        aliases = {3: 0}
    return pl.pallas_call(
        _fourier_body,
        out_shape=jax.ShapeDtypeStruct((N_TOK, FOURIER_DIM), BF16),
        grid=(n_seq,),
        in_specs=in_specs,
        out_specs=pl.BlockSpec((length, FOURIER_DIM), lambda b: (blk0 + b, 0)),
        input_output_aliases=aliases,
        compiler_params=_params(("arbitrary",)),
        name="fourier",
    )(*args)


def _rope_pair(x, cos, sin_signed):
    lane = lax.broadcasted_iota(jnp.int32, x.shape, 1)
    first_half = (lane & 31) < 16
    partner = jnp.where(first_half, pltpu.roll(x, LANES - 16, 1), pltpu.roll(x, 16, 1))
    return x * cos + partner * sin_signed


def _attn_body(*refs, rope, ctx):
    q_ref, kv_ref, kr_ref = refs[:3]
    refs = refs[3:]
    if rope:
        cq_ref, sq_ref, ck_ref, sk_ref = refs[:4]
        refs = refs[4:]
    if ctx:
        kvc_ref, krc_ref = refs[:2]
        refs = refs[2:]
    o_ref = refs[-1]

    def half_lanes(x):
        lane = lax.broadcasted_iota(jnp.int32, x.shape, 1)
        return jnp.where(lane < ROPE_DIM, x, 0.0).astype(BF16), jnp.where(lane >= ROPE_DIM, x, 0.0).astype(BF16)

    kr = kr_ref[...]
    kr = jnp.concatenate([kr, kr], axis=1)
    if rope:
        kr = _rope_pair(kr, ck_ref[...], sk_ref[...])
    kr_halves = half_lanes(kr)
    if ctx:
        krc = krc_ref[...]
        krc_halves = half_lanes(jnp.concatenate([krc, krc], axis=1))
    rope0 = MLA_HEADS * NOPE_DIM

    def keys_values(ref, c0, kr_half):
        k = jnp.concatenate([ref[:, c0:c0 + NOPE_DIM], kr_half], axis=1)
        v = ref[:, c0 + NOPE_DIM:c0 + NOPE_DIM + V_DIM]
        return k, jnp.concatenate([v, jnp.ones(v.shape, BF16)], axis=1)

    for pr in range(MLA_HEADS // 2):
        qr = q_ref[:, rope0 + pr * LANES:rope0 + (pr + 1) * LANES]
        if rope:
            qr = _rope_pair(qr, cq_ref[...], sq_ref[...])
        qr_halves = half_lanes(qr * ATTN_SCALE)
        for hh in range(2):
            h = 2 * pr + hh
            qn = (q_ref[:, h * NOPE_DIM:(h + 1) * NOPE_DIM] * ATTN_SCALE).astype(BF16)
            qcat = jnp.concatenate([qn, qr_halves[hh]], axis=1)
            c0 = h * (NOPE_DIM + V_DIM)
            k, v1 = keys_values(kv_ref, c0, kr_halves[hh])
            s = _dot_nt(qcat, k)
            m = jnp.max(s, axis=-1, keepdims=True)
            if ctx:
                kc, vc1 = keys_values(kvc_ref, c0, krc_halves[hh])
                sc = _dot_nt(qcat, kc)
                m = jnp.maximum(m, jnp.max(sc, axis=-1, keepdims=True))
            acc = _dot(jnp.exp(s - m).astype(BF16), v1)
            if ctx:
                acc = acc + _dot(jnp.exp(sc - m).astype(BF16), vc1)
            o_ref[:, h * V_DIM:(h + 1) * V_DIM] = (acc[:, :V_DIM] / acc[:, V_DIM:V_DIM + 1]).astype(BF16)


def _attention(q, kv, kr, *, n_seq, length, row0, tq, rope_tabs=None, ctx=None, prev=None):
    n_qt = length // tq
    qblk0 = row0 // tq
    kblk0 = row0 // length
    kv_w = MLA_HEADS * (NOPE_DIM + V_DIM)
    in_specs = [pl.BlockSpec((tq, MLA_HEADS * QK_DIM), lambda b, i: (qblk0 + b * n_qt + i, 0)),
                pl.BlockSpec((length, kv_w), lambda b, i: (kblk0 + b, 0)),
                pl.BlockSpec((length, ROPE_DIM), lambda b, i: (kblk0 + b, 0))]
    args = [q, kv, kr]
    if rope_tabs is not None:
        cos, sin = rope_tabs
        in_specs += [pl.BlockSpec((tq, LANES), lambda b, i: (i, 0)),
                     pl.BlockSpec((tq, LANES), lambda b, i: (i, 0)),
                     pl.BlockSpec((length, LANES), lambda b, i: (0, 0)),
                     pl.BlockSpec((length, LANES), lambda b, i: (0, 0))]
        args += [cos, sin, cos, sin]
    if ctx is not None:
        krc, cblk0 = ctx
        in_specs += [pl.BlockSpec((PAST_LEN, kv_w), lambda b, i: (cblk0 + b, 0)),
                     pl.BlockSpec((None, PAST_LEN, ROPE_DIM), lambda b, i: (b, 0, 0))]
        args += [kv, krc]
    aliases = {}
    if prev is not None:
        aliases = {len(args): 0}
        in_specs.append(ANY_SPEC)
        args.append(prev)
    return pl.pallas_call(
        functools.partial(_attn_body, rope=rope_tabs is not None, ctx=ctx is not None),
        out_shape=jax.ShapeDtypeStruct((N_TOK, MLA_HEADS * V_DIM), BF16),
        grid=(n_seq, n_qt),
        in_specs=in_specs,
        out_specs=pl.BlockSpec((tq, MLA_HEADS * V_DIM), lambda b, i: (qblk0 + b * n_qt + i, 0)),
        input_output_aliases=aliases,
        compiler_params=_params(("arbitrary", "arbitrary")),
        name="attention",
    )(*args)


def _rope_tables():
    rows = DEC_SEQ // GRID_W
    row = jnp.repeat(jnp.arange(rows, dtype=F32), GRID_W)
    col = jnp.tile(jnp.arange(GRID_W, dtype=F32), rows)
    inv = ROPE_THETA ** (-jnp.arange(ROPE_DIM // 4, dtype=F32) * 2.0 / (ROPE_DIM // 2))
    ang = jnp.stack([row[:, None] * inv, col[:, None] * inv], axis=1)
    cos, sin = jnp.cos(ang), jnp.sin(ang)
    cos64 = jnp.stack([cos, cos], axis=2).reshape(DEC_SEQ, ROPE_DIM)
    sin64 = jnp.stack([-sin, sin], axis=2).reshape(DEC_SEQ, ROPE_DIM)
    return jnp.tile(cos64, (1, 2)), jnp.tile(sin64, (1, 2))


_CONV_PAD = 8
_DIR_HEADS = 2 * HEADS_PER_GROUP


@functools.lru_cache(maxsize=None)
def _head_expand():
    e = np.zeros((_DIR_HEADS, 2 * GROUP_DIM), np.float32)
    for h in range(HEADS_PER_GROUP):
        e[h, h * SSM_HEAD_DIM:(h + 1) * SSM_HEAD_DIM] = 1.0
        e[HEADS_PER_GROUP + h, GROUP_DIM + h * SSM_HEAD_DIM:GROUP_DIM + (h + 1) * SSM_HEAD_DIM] = 1.0
    return e.astype(jnp.bfloat16)


def _scan_body(*refs, length, zero_init, emit_state, n_prev):
    (xr_ref, br_ref, cr_ref, wx_ref, wb_ref, wc_ref, bx_ref, bb_ref, bc_ref,
     dt_ref, bias_ref, acol_ref, dskip_ref, e_ref) = refs[:14]
    refs = refs[14:]
    if not zero_init:
        h0f_ref, h0b_ref = refs[:2]
        refs = refs[2:]
    refs = refs[n_prev:]
    y_ref = refs[0]
    refs = refs[1:]
    if emit_state:
        hf_ref, hb_ref = refs[:2]
        refs = refs[2:]
    padx, padb, padc, y_scr, bm_scr, cm_scr, xwf_scr, xwb_scr, decf_scr, decb_scr, tt_scr = refs
    q = SSM_CHUNK
    n_chunks = length // q
    hpg = HEADS_PER_GROUP
    head0 = pl.multiple_of(pl.program_id(1) * hpg, hpg)
    hd = SSM_HEAD_DIM
    state_shape = (GROUP_DIM, SSM_STATE)

    for raw, pad in ((xr_ref, padx), (br_ref, padb), (cr_ref, padc)):
        zeros = jnp.zeros((_CONV_PAD, pad.shape[1]), F32)
        pad[0:_CONV_PAD, :] = zeros
        pad[_CONV_PAD + length:, :] = zeros
        pad[_CONV_PAD:_CONV_PAD + length, :] = raw[...].astype(F32)

    rr = lax.broadcasted_iota(jnp.int32, (q, q), 0)
    cc = lax.broadcasted_iota(jnp.int32, (q, q), 1)
    lower = cc <= rr
    upper = cc >= rr
    tri_t = upper.astype(BF16)
    pair_lane = lax.broadcasted_iota(jnp.int32, (q, 2 * hd), 1)
    fwd_rows = lax.broadcasted_iota(jnp.int32, (_DIR_HEADS, q), 0) < hpg
    pad_rows = jnp.zeros((q - _DIR_HEADS, q), F32)
    a_neg = -jnp.exp(acol_ref[...])
    bias = bias_ref[...]
    e16 = e_ref[...]

    def conv_silu(pad, w_ref, b_ref, r0):
        n_win = q + 2 * _CONV_PAD
        win = pad[r0:r0 + n_win, :]
        acc = b_ref[...]
        for k in range(SSM_CONV):
            off = _CONV_PAD + k - SSM_CONV // 2
            tap = win[off:off + q, :] if off == _CONV_PAD else pltpu.roll(win, n_win - off, 0)[0:q, :]
            acc = acc + w_ref[k:k + 1, :] * tap
        return _silu(acc)

    totals = []
    for c in range(n_chunks):
        r0 = c * q
        rows = slice(r0, r0 + q)
        x = conv_silu(padx, wx_ref, bx_ref, r0)
        bm = conv_silu(padb, wb_ref, bb_ref, r0).astype(BF16)
        cm = conv_silu(padc, wc_ref, bc_ref, r0).astype(BF16)
        bm_scr[rows, :] = bm
        cm_scr[rows, :] = cm
        tt_scr[c] = (dt_ref[rows, :] + bias).T
        t = jnp.concatenate([tt_scr[c, pl.ds(head0, hpg), :], tt_scr[c, pl.ds(SSM_HEADS + head0, hpg), :]],
                            axis=0)
        dt_t = jnp.maximum(t, 0.0) + jnp.log1p(jnp.exp(-jnp.abs(t)))
        a_t = dt_t * a_neg
        a_hi = a_t.astype(BF16)
        r1 = a_t - a_hi.astype(F32)
        a_mid = r1.astype(BF16)
        a_lo = (r1 - a_mid.astype(F32)).astype(BF16)
        cs_t = _dot(a_hi, tri_t) + _dot(a_mid, tri_t) + _dot(a_lo, tri_t)
        ecs_t = cs_t - a_t
        total = cs_t[:, q - 1:q]
        totals.append(jnp.exp(total))
        cs = jnp.concatenate([cs_t, pad_rows], axis=0).T
        ecs = jnp.concatenate([ecs_t, pad_rows], axis=0).T
        cb = _dot_nt(cm, bm)
        pieces = []
        for pr in range(hpg // 2):
            mats = []
            for h in (2 * pr, 2 * pr + 1):
                lf = jnp.exp(jnp.where(lower, cs[:, h:h + 1] - cs_t[h:h + 1, :], NEG_BIG))
                lb = jnp.exp(jnp.where(upper, ecs_t[hpg + h:hpg + h + 1, :] - ecs[:, hpg + h:hpg + h + 1], NEG_BIG))
                mats.append((cb * (lf * dt_t[h:h + 1, :] + lb * dt_t[hpg + h:hpg + h + 1, :])).astype(BF16))
            xp = x[:, pr * 2 * hd:(pr + 1) * 2 * hd]
            rhs = jnp.concatenate([jnp.where(pair_lane < hd, xp, 0.0), jnp.where(pair_lane >= hd, xp, 0.0)], axis=0)
            pieces.append(_dot(jnp.concatenate(mats, axis=1), rhs.astype(BF16)))
        y_scr[rows, :] = jnp.concatenate(pieces, axis=1) + dskip_ref[...] * x
        dec = _dot_tn(jnp.exp(jnp.where(fwd_rows, cs_t, total - ecs_t)).astype(BF16), e16)
        wgt = _dot_tn((jnp.exp(jnp.where(fwd_rows, total - cs_t, ecs_t)) * dt_t).astype(BF16), e16)
        decf_scr[rows, :] = dec[:, :GROUP_DIM].astype(BF16)
        decb_scr[rows, :] = dec[:, GROUP_DIM:].astype(BF16)
        xwf_scr[rows, :] = (x * wgt[:, :GROUP_DIM]).astype(BF16)
        xwb_scr[rows, :] = (x * wgt[:, GROUP_DIM:]).astype(BF16)

    def step(state, c, dec_scr, xw_scr, row0):
        rows = slice(c * q, (c + 1) * q)
        y_off = _dot_nt(cm_scr[rows, :], state.astype(BF16))
        y_scr[rows, :] = y_scr[rows, :] + y_off * dec_scr[rows, :].astype(F32)
        upd = _dot_tn(xw_scr[rows, :], bm_scr[rows, :])
        decay = totals[c]
        return jnp.concatenate(
            [state[h * hd:(h + 1) * hd, :] * decay[row0 + h:row0 + h + 1, :] + upd[h * hd:(h + 1) * hd, :]
             for h in range(hpg)], axis=0)

    if zero_init:
        hf = jnp.zeros(state_shape, F32)
        hb = jnp.zeros(state_shape, F32)
    else:
        hf = h0f_ref[...].reshape(state_shape)
        hb = h0b_ref[...].reshape(state_shape)
    for i in range(n_chunks):
        hf = step(hf, i, decf_scr, xwf_scr, 0)
        hb = step(hb, n_chunks - 1 - i, decb_scr, xwb_scr, hpg)
    y_ref[...] = y_scr[...].astype(BF16)
    if emit_state:
        hf_ref[...] = hf.reshape(hf_ref.shape)
        hb_ref[...] = hb.reshape(hb_ref.shape)


def _scan(xbc, dt, conv_w, conv_b, dt_bias, alog_col, dskip, layer, *, n_seq, length, row0, h0=None,
          y_prev=None, state_prev=None, emit_state):
    blk0 = row0 // length
    b0 = SSM_INNER // SSM_STATE
    c0 = b0 + SSM_GROUPS
    seq = lambda width, col: pl.BlockSpec((length, width), lambda b, g: (blk0 + b, col + g))
    cw = lambda width, col: pl.BlockSpec((None, SSM_CONV, width), lambda b, g: (layer, 0, col + g))
    cbias = lambda width, col: pl.BlockSpec((None, 1, width), lambda b, g: (layer, 0, col + g))
    grp = lambda width: pl.BlockSpec((None, 1, width), lambda b, g: (layer, 0, g))
    h0_spec = pl.BlockSpec((None, None, HEADS_PER_GROUP, SSM_HEAD_DIM, SSM_STATE), lambda b, g: (b, layer, g, 0, 0))
    in_specs = [seq(GROUP_DIM, 0), seq(SSM_STATE, b0), seq(SSM_STATE, c0),
                cw(GROUP_DIM, 0), cw(SSM_STATE, b0), cw(SSM_STATE, c0),
                cbias(GROUP_DIM, 0), cbias(SSM_STATE, b0), cbias(SSM_STATE, c0),
                pl.BlockSpec((length, 2 * SSM_HEADS), lambda b, g: (blk0 + b, 0)),
                pl.BlockSpec((None, 1, 2 * SSM_HEADS), lambda b, g: (layer, 0, 0)),
                pl.BlockSpec((None, None, _DIR_HEADS, 1), lambda b, g: (layer, g, 0, 0)), grp(GROUP_DIM),
                pl.BlockSpec((_DIR_HEADS, 2 * GROUP_DIM), lambda b, g: (0, 0))]
    cb3 = conv_b.reshape(N_ODD, 1, SSM_CONV_DIM)
    args = [xbc, xbc, xbc, conv_w, conv_w, conv_w, cb3, cb3, cb3, dt, dt_bias, alog_col, dskip,
            jnp.asarray(_head_expand())]
    if h0 is not None:
        in_specs += [h0_spec, h0_spec]
        args += list(h0)
    out_shape = [jax.ShapeDtypeStruct((N_TOK, SSM_INNER), BF16)]
    out_specs = [seq(GROUP_DIM, 0)]
    if emit_state:
        st = jax.ShapeDtypeStruct((n_seq, N_ODD, SSM_HEADS, SSM_HEAD_DIM, SSM_STATE), F32)
        out_shape += [st, st]
        out_specs += [h0_spec, h0_spec]
    aliases = {}
    prevs = ([] if y_prev is None else [y_prev]) + ([] if state_prev is None else list(state_prev))
    for k, prev in enumerate(prevs):
        aliases[len(args)] = k if y_prev is not None else k + 1
        in_specs.append(ANY_SPEC)
        args.append(prev)
    pad_rows = length + 2 * _CONV_PAD
    scratch = [pltpu.VMEM((pad_rows, GROUP_DIM), F32), pltpu.VMEM((pad_rows, SSM_STATE), F32),
               pltpu.VMEM((pad_rows, SSM_STATE), F32), pltpu.VMEM((length, GROUP_DIM), F32),
               pltpu.VMEM((length, SSM_STATE), BF16), pltpu.VMEM((length, SSM_STATE), BF16),
               pltpu.VMEM((length, GROUP_DIM), BF16), pltpu.VMEM((length, GROUP_DIM), BF16),
               pltpu.VMEM((length, GROUP_DIM), BF16), pltpu.VMEM((length, GROUP_DIM), BF16),
               pltpu.VMEM((length // SSM_CHUNK, 2 * SSM_HEADS, SSM_CHUNK), F32)]
    return pl.pallas_call(
        functools.partial(_scan_body, length=length, zero_init=h0 is None, emit_state=emit_state,
                          n_prev=len(prevs)),
        out_shape=out_shape,
        grid=(n_seq, SSM_GROUPS),
        in_specs=in_specs,
        out_specs=out_specs,
        scratch_shapes=scratch,
        input_output_aliases=aliases,
        compiler_params=_params(("arbitrary", "arbitrary")),
        name="ssd_scan",
    )(*args)


def _gate_norm_body(y_ref, z_ref, w_ref, o_ref):
    g = y_ref[...].astype(F32) * _silu(z_ref[...].astype(F32))
    o_ref[...] = _rms(g, w_ref[...]).astype(BF16)


def _gate_norm(y, z, norm_w, layer, tm=256):
    return pl.pallas_call(
        _gate_norm_body,
        out_shape=jax.ShapeDtypeStruct((N_TOK, SSM_INNER), BF16),
        grid=(N_TOK // tm,),
        in_specs=[pl.BlockSpec((tm, SSM_INNER), lambda t: (t, 0)),
                  pl.BlockSpec((tm, SSM_INNER), lambda t: (t, 0)),
                  pl.BlockSpec((None, 1, SSM_INNER), lambda t: (layer, 0, 0))],
        out_specs=pl.BlockSpec((tm, SSM_INNER), lambda t: (t, 0)),
        compiler_params=_params(("arbitrary",)),
        name="gate_norm",
    )(y, z, norm_w.reshape(N_ODD, 1, SSM_INNER))


def _moe_up_body(e_ref, n_ref, m_ref, first_ref, valid_ref, x_ref, wg_ref, wu_ref, o_ref, gb, ub):
    s = pl.program_id(0)

    @pl.when(first_ref[s] == 1)
    def _():
        gb[...] = wg_ref[...].astype(BF16)
        ub[...] = wu_ref[...].astype(BF16)

    def emit(rows):
        x = x_ref[rows, :]
        o_ref[rows, :] = (_silu(_dot(x, gb[...])) * _dot(x, ub[...])).astype(BF16)

    @pl.when(valid_ref[s] == _TILE_FULL)
    def _():
        emit(slice(None))

    @pl.when(valid_ref[s] == _TILE_HALF)
    def _():
        emit(slice(0, MOE_TM // 2))


def _moe_down_body(e_ref, n_ref, m_ref, first_ref, valid_ref, h_ref, w_ref, o_ref, wb):
    s = pl.program_id(0)

    @pl.when(first_ref[s] == 1)
    def _():
        wb[...] = w_ref[...].astype(BF16)

    def emit(rows):
        o_ref[rows, :] = _dot(h_ref[rows, :], wb[...]).astype(BF16)

    @pl.when(valid_ref[s] == _TILE_FULL)
    def _():
        emit(slice(None))

    @pl.when(valid_ref[s] == _TILE_HALF)
    def _():
        emit(slice(0, MOE_TM // 2))


_TILE_SKIP, _TILE_FULL, _TILE_HALF = 0, 1, 2


def _moe_tables(tiles, count, n_col_tiles):
    n_steps = MOE_TILES * n_col_tiles
    tile_start = jnp.cumsum(tiles) - tiles
    step_end = jnp.cumsum(tiles * n_col_tiles)
    total = step_end[-1]
    s = jnp.minimum(jnp.arange(n_steps, dtype=jnp.int32), total - 1)
    e = jnp.sum((s[:, None] >= step_end[None, :]).astype(jnp.int32), axis=1)
    local = s - (step_end - tiles * n_col_tiles)[e]
    te = jnp.maximum(tiles[e], 1)
    n = local // te
    k = local - n * te
    m = tile_start[e] + k
    live = jnp.arange(n_steps) < total
    half = (count[e] - k * MOE_TM) <= MOE_TM // 2
    kind = jnp.where(live, jnp.where(half, _TILE_HALF, _TILE_FULL), _TILE_SKIP).astype(jnp.int32)
    first = ((k == 0) & live).astype(jnp.int32)
    return e.astype(jnp.int32), n.astype(jnp.int32), m.astype(jnp.int32), first, kind


def _moe(u, idx, w_gate, w_up, w_down, layer):
    tm = MOE_TM
    flat_e = idx.reshape(-1)
    onehot = (flat_e[:, None] == jnp.arange(N_EXPERTS)[None, :]).astype(jnp.int32)
    rank = jnp.sum((jnp.cumsum(onehot, axis=0) - onehot) * onehot, axis=1)
    count = jnp.sum(onehot, axis=0)
    tiles = (count + tm - 1) // tm
    row_start = (jnp.cumsum(tiles) - tiles) * tm
    dest = row_start[flat_e] + rank
    token = jnp.arange(N_TOK * TOP_K, dtype=jnp.int32) // TOP_K
    src = jnp.zeros((MOE_ROWS,), jnp.int32).at[dest].set(token, mode="promise_in_bounds", unique_indices=True)
    x_sorted = u.at[src].get(mode="promise_in_bounds")

    tn = 1024
    n_up = EXPERT_DIM // tn
    tabs = _moe_tables(tiles, count, n_up)
    w_spec = pl.BlockSpec((None, None, D_MODEL, tn), lambda s, e, n, m, f, v: (layer, e[s], 0, n[s]))
    hmid = pl.pallas_call(
        _moe_up_body,
        out_shape=jax.ShapeDtypeStruct((MOE_ROWS, EXPERT_DIM), BF16),
        grid_spec=pltpu.PrefetchScalarGridSpec(
            num_scalar_prefetch=5,
            grid=(MOE_TILES * n_up,),
            in_specs=[pl.BlockSpec((tm, D_MODEL), lambda s, e, n, m, f, v: (m[s], 0)), w_spec, w_spec],
            out_specs=pl.BlockSpec((tm, tn), lambda s, e, n, m, f, v: (m[s], n[s])),
            scratch_shapes=[pltpu.VMEM((D_MODEL, tn), BF16), pltpu.VMEM((D_MODEL, tn), BF16)]),
        compiler_params=_params(("arbitrary",)),
        name="moe_up",
    )(*tabs, x_sorted, w_gate, w_up)

    tn = 1024
    n_dn = D_MODEL // tn
    tabs = _moe_tables(tiles, count, n_dn)
    y_sorted = pl.pallas_call(
        _moe_down_body,
        out_shape=jax.ShapeDtypeStruct((MOE_ROWS, D_MODEL), BF16),
        grid_spec=pltpu.PrefetchScalarGridSpec(
            num_scalar_prefetch=5,
            grid=(MOE_TILES * n_dn,),
            in_specs=[pl.BlockSpec((tm, EXPERT_DIM), lambda s, e, n, m, f, v: (m[s], 0)),
                      pl.BlockSpec((None, None, EXPERT_DIM, tn), lambda s, e, n, m, f, v: (layer, e[s], 0, n[s]))],
            out_specs=pl.BlockSpec((tm, tn), lambda s, e, n, m, f, v: (m[s], n[s])),
            scratch_shapes=[pltpu.VMEM((EXPERT_DIM, tn), BF16)]),
        compiler_params=_params(("arbitrary",), vmem_limit=56 * 1024 * 1024),
        name="moe_down",
    )(*tabs, hmid, w_down)

    dest2 = dest.reshape(N_TOK, TOP_K)
    return (y_sorted.at[dest2[:, 0]].get(mode="promise_in_bounds"),
            y_sorted.at[dest2[:, 1]].get(mode="promise_in_bounds"))


def _q_perm():
    h = np.arange(MLA_HEADS)[:, None]
    nope = (h * QK_DIM + np.arange(NOPE_DIM)[None, :]).reshape(-1)
    rope = (h * QK_DIM + NOPE_DIM + np.arange(ROPE_DIM)[None, :]).reshape(-1)
    return np.concatenate([nope, rope])


def _even_mixer(u, j, p, cache_ckv, cache_krope, rope_tabs):
    proj = _matmul([u], [p['mix_w_in']], layer=j, n_cols=MIX_IN, tn=MIX_IN, tm=512, out_dtype=F32, name="mix_in")
    qn, ckv, kr = _split_proj(proj, p['q_norm_w'], p['kv_norm_w'], j)
    f = _fourier(proj, BATCH, SEQ, 0)
    f = _fourier(proj, DEC_BATCH, DEC_SEQ, N_TOK_P, prev=f)
    q = _matmul([qn], [p['w_q_perm']], layer=j, n_cols=MLA_HEADS * QK_DIM, tn=MLA_HEADS * QK_DIM, tm=1024,
                out_dtype=F32, name="q_proj")
    ckv_all = jnp.concatenate([ckv, cache_ckv[:, j].reshape(DEC_BATCH * PAST_LEN, KV_RANK)], axis=0).astype(BF16)
    kv = _matmul([ckv_all], [p['w_kv_b']], layer=j, n_cols=MLA_HEADS * (NOPE_DIM + V_DIM),
                 tn=MLA_HEADS * (NOPE_DIM + V_DIM), tm=1024, out_dtype=BF16, name="kv_proj")
    o = _attention(q, kv, kr, n_seq=BATCH, length=SEQ, row0=0, tq=SEQ)
    o = _attention(q, kv, kr, n_seq=DEC_BATCH, length=DEC_SEQ, row0=N_TOK_P, tq=256, rope_tabs=rope_tabs,
                   ctx=(cache_krope[:, j], N_TOK // PAST_LEN), prev=o)
    h = _matmul([f, o], [p['mix_w_out']], layer=j, n_cols=D_MODEL, tn=1024, tm=1024, out_dtype=BF16, name="mix_out")
    return h, ckv, kr


def _group_head_rows(w):
    w = w.reshape(N_ODD, 2, SSM_GROUPS, HEADS_PER_GROUP)
    return jnp.moveaxis(w, 1, 2).reshape(N_ODD, SSM_GROUPS, _DIR_HEADS, 1)


def _odd_mixer(u, j, p, state_f, state_b, state_prev):
    z = _matmul([u], [p['ssm_w_in']], layer=j, n_cols=SSM_INNER, tn=1024, tm=1024, out_dtype=BF16, name="ssm_in_z")
    xbc = _matmul([u], [p['ssm_w_in']], layer=j, n_cols=SSM_CONV_DIM, col0=SSM_INNER, tn=1024, tm=1024,
                  out_dtype=BF16, name="ssm_in_xbc")
    dt = _matmul([u], [p['ssm_w_in']], layer=j, n_cols=2 * SSM_HEADS, col0=SSM_INNER + SSM_CONV_DIM, tn=2 * SSM_HEADS,
                 tm=1024, out_dtype=F32, name="ssm_dt")
    common = (xbc, dt, p['ssm_conv_w'], p['ssm_conv_b'], p['ssm_dt_bias'], p['ssm_alog_col'], p['ssm_dskip'], j)
    y, hf, hb = _scan(*common, n_seq=BATCH, length=SEQ, row0=0, state_prev=state_prev, emit_state=True)
    (y,) = _scan(*common, n_seq=DEC_BATCH, length=DEC_SEQ, row0=N_TOK_P, h0=(state_f, state_b), y_prev=y,
                 emit_state=False)
    yn = _gate_norm(y, z, p['ssm_norm_w'], j)
    h = _matmul([yn], [p['ssm_w_out']], layer=j, n_cols=D_MODEL, tn=512, tm=1024, out_dtype=BF16, name="ssm_out")
    return h, hf, hb


def kernel(x_prompt, x_sample, cache_ckv, cache_krope, state_ssm_fwd, state_ssm_bwd, c, c_ctx, ada_w, ada_b, ln1_g, ln1_b, ln2_g, ln2_b, mix_w_in, q_norm_w, w_q_b, kv_norm_w, w_kv_b, mix_w_out, ffn_w_gate, ffn_w_up, ffn_w_down, ssm_w_in, ssm_conv_w, ssm_conv_b, ssm_dt_bias, ssm_a_log, ssm_d, ssm_norm_w, ssm_w_out, moe_router, moe_w_gate, moe_w_up, moe_w_down):
    p = dict(mix_w_in=mix_w_in, q_norm_w=q_norm_w, w_q_perm=w_q_b[:, :, _q_perm()], kv_norm_w=kv_norm_w,
             w_kv_b=w_kv_b, mix_w_out=mix_w_out, ssm_w_in=ssm_w_in, ssm_conv_w=ssm_conv_w, ssm_conv_b=ssm_conv_b,
             ssm_dt_bias=ssm_dt_bias.reshape(N_ODD, 1, 2 * SSM_HEADS),
             ssm_alog_col=_group_head_rows(ssm_a_log),
             ssm_dskip=jnp.repeat(ssm_d, SSM_HEAD_DIM, axis=1).reshape(N_ODD, 1, SSM_INNER),
             ssm_norm_w=ssm_norm_w, ssm_w_out=ssm_w_out)
    cond = jnp.concatenate([c_ctx[None, :], c, jnp.zeros((COND_ROWS - N_COND, D_MODEL), F32)], axis=0)
    mod = _ada(cond, ada_w, ada_b)
    mod5 = mod[:, :N_COND].reshape(DEPTH, N_COND, N_MOD, 1, D_MODEL)
    rope_tabs = _rope_tables()

    x, u = _modulate(x_prompt, x_sample, mod5, 0)
    ckvs, krs = [], []
    states = None
    for i in range(DEPTH):
        j = i // 2
        if i % 2 == 0:
            h, ckv, kr = _even_mixer(u, j, p, cache_ckv, cache_krope, rope_tabs)
            ckvs.append(ckv[:N_TOK_P].reshape(BATCH, SEQ, KV_RANK))
            krs.append(kr[:N_TOK_P].reshape(BATCH, SEQ, ROPE_DIM))
            x, u = _ln(x, h, mod5, ln1_g, ln1_b, layer=i, gate_chunk=2, mod_next=(i, 4, 3))
            hmid = _matmul([u], [ffn_w_gate, ffn_w_up], layer=j, n_cols=FFN_DIM, tn=512, tm=1024, out_dtype=BF16,
                           swiglu=True, name="ffn_up")
            f = _matmul([hmid], [ffn_w_down], layer=j, n_cols=D_MODEL, tn=512, tm=512, out_dtype=BF16,
                        name="ffn_down")
            pair_probs = None
        else:
            h, hf, hb = _odd_mixer(u, j, p, state_ssm_fwd, state_ssm_bwd, states)
            states = (hf, hb)
            router = jnp.pad(moe_router[j], ((0, 0), (0, LANES - N_EXPERTS)))
            x, u, pair_probs, idx = _ln(x, h, mod5, ln1_g, ln1_b, layer=i, gate_chunk=2, mod_next=(i, 4, 3),
                                        router=router)
            f = _moe(u, idx[:, :TOP_K], moe_w_gate, moe_w_up, moe_w_down, j)
        if i + 1 < DEPTH:
            x, u = _ln(x, f, mod5, ln2_g, ln2_b, layer=i, gate_chunk=5, mod_next=(i + 1, 1, 0),
                       pair_probs=pair_probs)
        else:
            y_prompt, y_sample = _ln(x, f, mod5, ln2_g, ln2_b, layer=i, gate_chunk=5, pair_probs=pair_probs)
    return (y_prompt.reshape(BATCH, SEQ, D_MODEL), y_sample.reshape(DEC_BATCH, DEC_SEQ, D_MODEL),
            jnp.stack(ckvs, axis=1), jnp.stack(krs, axis=1), states[0], states[1])
```

```python
import functools

import numpy as np
import jax
import jax.numpy as jnp
from jax import lax
from jax.experimental import pallas as pl
from jax.experimental.pallas import tpu as pltpu

D_MODEL = 2048
BATCH = 16
SEQ = 256
DEPTH = 4
DEC_BATCH = 2
DEC_SEQ = 1024
PAST_LEN = 512
GRID_W = 64
N_EVEN = (DEPTH + 1) // 2
N_ODD = DEPTH // 2
FOURIER_GROUPS = 4
FOURIER_GROUP_DIM = 128
FOURIER_DIM = FOURIER_GROUPS * FOURIER_GROUP_DIM
MLA_HEADS = 12
Q_RANK = 768
KV_RANK = 256
NOPE_DIM = 128
ROPE_DIM = 64
V_DIM = 128
QK_DIM = NOPE_DIM + ROPE_DIM
ATTN_SCALE = QK_DIM ** -0.5
ROPE_THETA = 10000.0
MIX_IN = FOURIER_DIM + Q_RANK + KV_RANK + ROPE_DIM
SSM_INNER = 2 * D_MODEL
SSM_HEAD_DIM = 64
SSM_HEADS = SSM_INNER // SSM_HEAD_DIM
SSM_GROUPS = 8
SSM_STATE = 128
SSM_CONV = 5
SSM_CHUNK = 128
SSM_CONV_DIM = SSM_INNER + 2 * SSM_GROUPS * SSM_STATE
FFN_DIM = 5632
N_EXPERTS = 8
TOP_K = 2
EXPERT_DIM = 4096
ALPHA = (2 * DEPTH) ** 0.25
EPS = 1e-5

F32 = jnp.float32
BF16 = jnp.bfloat16

N_TOK_P = BATCH * SEQ
N_TOK_S = DEC_BATCH * DEC_SEQ
N_TOK = N_TOK_P + N_TOK_S
N_MOD = 6
N_COND = 1 + DEC_BATCH
COND_ROWS = 8
HEADS_PER_GROUP = SSM_HEADS // SSM_GROUPS
GROUP_DIM = HEADS_PER_GROUP * SSM_HEAD_DIM
LANES = 128
V7X_VMEM_LIMIT = 52 * 1024 * 1024
NEG_BIG = -1e30

MOE_TM = 512
MOE_TILES = (N_TOK * TOP_K) // MOE_TM + N_EXPERTS
MOE_ROWS = MOE_TILES * MOE_TM

ANY_SPEC = pl.BlockSpec(memory_space=pl.ANY)


def _params(sem, vmem_limit=V7X_VMEM_LIMIT):
    return pltpu.CompilerParams(dimension_semantics=sem, vmem_limit_bytes=vmem_limit)


def _silu(x):
    return x * jax.nn.sigmoid(x)


def _split2(x):
    hi = x.astype(BF16)
    lo = (x - hi.astype(F32)).astype(BF16)
    return hi, lo


def _dot(a, b):
    return jnp.dot(a, b, preferred_element_type=F32)


def _dot_nt(a, b):
    return lax.dot_general(a, b, (((1,), (1,)), ((), ())), preferred_element_type=F32)


def _dot_tn(a, b):
    return lax.dot_general(a, b, (((0,), (0,)), ((), ())), preferred_element_type=F32)


def _dot3(a, b_hi, b_lo):
    a_hi, a_lo = _split2(a)
    return _dot(a_hi, b_hi) + _dot(a_lo, b_hi) + _dot(a_hi, b_lo)


def _cond_of_tile(t, tm):
    p_tiles = N_TOK_P // tm
    per_seq = DEC_SEQ // tm
    return jnp.where(t < p_tiles, 0, 1 + (t - p_tiles) // per_seq)


def _mod_spec(layer, chunk, tm):
    return pl.BlockSpec((None, None, None, 1, D_MODEL),
                        lambda t: (layer, _cond_of_tile(t, tm), chunk, 0, 0))


def _ada_body(c_ref, w_ref, b_ref, o_ref):
    s = _silu(c_ref[...]).astype(BF16)
    o_ref[...] = _dot(s, w_ref[...].astype(BF16)) + b_ref[...]


def _ada(cond, ada_w, ada_b):
    tn = 1024
    n_out = N_MOD * D_MODEL
    return pl.pallas_call(
        _ada_body,
        out_shape=jax.ShapeDtypeStruct((DEPTH, COND_ROWS, n_out), F32),
        grid=(DEPTH, n_out // tn),
        in_specs=[pl.BlockSpec((COND_ROWS, D_MODEL), lambda i, n: (0, 0)),
                  pl.BlockSpec((None, D_MODEL, tn), lambda i, n: (i, 0, n)),
                  pl.BlockSpec((None, 1, tn), lambda i, n: (i, 0, n))],
        out_specs=pl.BlockSpec((None, COND_ROWS, tn), lambda i, n: (i, 0, n)),
        compiler_params=_params(("arbitrary", "arbitrary")),
        name="ada",
    )(cond, ada_w, ada_b.reshape(DEPTH, 1, n_out))


def _mm_body(*refs, n_x, n_w, swiglu):
    x_refs = refs[:n_x]
    w_refs = refs[n_x:n_x + n_w]
    o_ref = refs[n_x + n_w]
    wb_refs = refs[n_x + n_w + 1:]

    @pl.when(pl.program_id(1) == 0)
    def _():
        for w_ref, wb in zip(w_refs, wb_refs):
            wb[...] = w_ref[...].astype(BF16)

    if n_x == 1:
        x = x_refs[0][...]
    else:
        x = jnp.concatenate([r[...] for r in x_refs], axis=1)
    accs = [_dot(x, wb[...]) for wb in wb_refs]
    out = _silu(accs[0]) * accs[1] if swiglu else accs[0]
    o_ref[...] = out.astype(o_ref.dtype)


def _matmul(xs, ws, *, layer, n_cols, tn, tm, out_dtype, col0=0, swiglu=False, name):
    m = xs[0].shape[0]
    k = sum(x.shape[1] for x in xs)
    assert m % tm == 0 and n_cols % tn == 0 and col0 % tn == 0 and all(w.shape[1] == k for w in ws)
    cb0 = col0 // tn
    in_specs = [pl.BlockSpec((tm, x.shape[1]), lambda n, r: (r, 0)) for x in xs]
    in_specs += [pl.BlockSpec((None, k, tn), lambda n, r: (layer, 0, cb0 + n)) for _ in ws]
    return pl.pallas_call(
        functools.partial(_mm_body, n_x=len(xs), n_w=len(ws), swiglu=swiglu),
        out_shape=jax.ShapeDtypeStruct((m, n_cols), out_dtype),
        grid=(n_cols // tn, m // tm),
        in_specs=in_specs,
        out_specs=pl.BlockSpec((tm, tn), lambda n, r: (r, n)),
        scratch_shapes=[pltpu.VMEM((k, tn), BF16) for _ in ws],
        compiler_params=_params(("arbitrary", "arbitrary")),
        name=name,
    )(*xs, *ws)


_LN_TM = 512
_P_TILES = N_TOK_P // _LN_TM


def _modulate_body(xp_ref, xs_ref, sc_ref, sh_ref, x_ref, u_ref):
    def emit(x):
        x_ref[...] = x
        u_ref[...] = (x * (1.0 + sc_ref[...]) + sh_ref[...]).astype(BF16)

    @pl.when(pl.program_id(0) < _P_TILES)
    def _():
        emit(xp_ref[...])

    @pl.when(pl.program_id(0) >= _P_TILES)
    def _():
        emit(xs_ref[...])


def _modulate(x_prompt, x_sample, mod5, layer):
    tm = _LN_TM
    row = pl.BlockSpec((tm, D_MODEL), lambda t: (t, 0))
    return pl.pallas_call(
        _modulate_body,
        out_shape=[jax.ShapeDtypeStruct((N_TOK, D_MODEL), F32), jax.ShapeDtypeStruct((N_TOK, D_MODEL), BF16)],
        grid=(N_TOK // tm,),
        in_specs=[pl.BlockSpec((tm, D_MODEL), lambda t: (jnp.minimum(t, _P_TILES - 1), 0)),
                  pl.BlockSpec((tm, D_MODEL), lambda t: (jnp.maximum(t - _P_TILES, 0), 0)),
                  _mod_spec(layer, 1, tm), _mod_spec(layer, 0, tm)],
        out_specs=[row, row],
        compiler_params=_params(("arbitrary",)),
        name="modulate",
    )(x_prompt.reshape(N_TOK_P, D_MODEL), x_sample.reshape(N_TOK_S, D_MODEL), mod5, mod5)


def _ln_body(*refs, modulate, route, gated_pair):
    if gated_pair:
        x_ref, h_ref, h2_ref, pr_ref, g_ref, lg_ref, lb_ref = refs[:7]
        refs = refs[7:]
        pr = pr_ref[...]
        h = h_ref[...].astype(F32) * pr[:, 0:1] + h2_ref[...].astype(F32) * pr[:, 1:2]
    else:
        x_ref, h_ref, g_ref, lg_ref, lb_ref = refs[:5]
        refs = refs[5:]
        h = h_ref[...].astype(F32)
    v = ALPHA * x_ref[...] + g_ref[...] * h
    mu = jnp.mean(v, axis=-1, keepdims=True)
    d = v - mu
    var = jnp.mean(d * d, axis=-1, keepdims=True)
    xn = d * lax.rsqrt(var + EPS) * lg_ref[...] + lb_ref[...]
    if not modulate:
        yp_ref, ys_ref = refs

        @pl.when(pl.program_id(0) < _P_TILES)
        def _():
            yp_ref[...] = xn

        @pl.when(pl.program_id(0) >= _P_TILES)
        def _():
            ys_ref[...] = xn
        return
    sc_ref, sh_ref = refs[:2]
    refs = refs[2:]
    u = xn * (1.0 + sc_ref[...]) + sh_ref[...]
    if not route:
        xo_ref, uo_ref = refs
        xo_ref[...] = xn
        uo_ref[...] = u.astype(BF16)
        return
    r_ref, xo_ref, uo_ref, p_ref, i_ref = refs
    xo_ref[...] = xn
    uo_ref[...] = u.astype(BF16)
    r_hi, r_lo = _split2(r_ref[...])
    logits = _dot3(u, r_hi, r_lo)
    lane = lax.broadcasted_iota(jnp.int32, logits.shape, 1)
    l1 = jnp.where(lane < N_EXPERTS, logits, -jnp.inf)
    m1 = jnp.max(l1, axis=-1, keepdims=True)
    i1 = jnp.min(jnp.where(l1 == m1, lane, LANES), axis=-1, keepdims=True)
    l2 = jnp.where(lane == i1, -jnp.inf, l1)
    m2 = jnp.max(l2, axis=-1, keepdims=True)
    i2 = jnp.min(jnp.where(l2 == m2, lane, LANES), axis=-1, keepdims=True)
    e = jnp.exp(m2 - m1)
    p1 = 1.0 / (1.0 + e)
    p2 = e / (1.0 + e)
    p_ref[...] = jnp.where(lane == 0, p1, jnp.where(lane == 1, p2, 0.0))
    i_ref[...] = jnp.where(lane == 0, i1, jnp.where(lane == 1, i2, 0))


def _ln(x, h, mod5, ln_g, ln_b, *, layer, gate_chunk, mod_next=None, router=None, pair_probs=None):
    tm = _LN_TM
    row = pl.BlockSpec((tm, D_MODEL), lambda t: (t, 0))
    vec = pl.BlockSpec((None, 1, D_MODEL), lambda t: (layer, 0, 0))
    lane_row = pl.BlockSpec((tm, LANES), lambda t: (t, 0))
    if pair_probs is None:
        in_specs = [row, row]
        args = [x, h]
    else:
        in_specs = [row, row, row, lane_row]
        args = [x, h[0], h[1], pair_probs]
    in_specs += [_mod_spec(layer, gate_chunk, tm), vec, vec]
    args += [mod5, ln_g.reshape(DEPTH, 1, D_MODEL), ln_b.reshape(DEPTH, 1, D_MODEL)]
    if mod_next is None:
        out_shape = [jax.ShapeDtypeStruct((N_TOK_P, D_MODEL), F32), jax.ShapeDtypeStruct((N_TOK_S, D_MODEL), F32)]
        out_specs = [pl.BlockSpec((tm, D_MODEL), lambda t: (jnp.minimum(t, _P_TILES - 1), 0)),
                     pl.BlockSpec((tm, D_MODEL), lambda t: (jnp.maximum(t - _P_TILES, 0), 0))]
    else:
        nl, sc_chunk, sh_chunk = mod_next
        in_specs += [_mod_spec(nl, sc_chunk, tm), _mod_spec(nl, sh_chunk, tm)]
        args += [mod5, mod5]
        out_shape = [jax.ShapeDtypeStruct((N_TOK, D_MODEL), F32), jax.ShapeDtypeStruct((N_TOK, D_MODEL), BF16)]
        out_specs = [row, row]
    if router is not None:
        in_specs.append(pl.BlockSpec((D_MODEL, LANES), lambda t: (0, 0)))
        args.append(router)
        out_shape += [jax.ShapeDtypeStruct((N_TOK, LANES), F32), jax.ShapeDtypeStruct((N_TOK, LANES), jnp.int32)]
        out_specs += [lane_row, lane_row]
    return pl.pallas_call(
        functools.partial(_ln_body, modulate=mod_next is not None, route=router is not None,
                          gated_pair=pair_probs is not None),
        out_shape=out_shape,
        grid=(N_TOK // tm,),
        in_specs=in_specs,
        out_specs=out_specs,
        compiler_params=_params(("arbitrary",)),
        name="ln",
    )(*args)


def _rms(x, w):
    return x * lax.rsqrt(jnp.mean(x * x, axis=-1, keepdims=True) + EPS) * w


def _split_body(p_ref, qw_ref, kw_ref, qn_ref, ckv_ref, kr_ref):
    q0 = FOURIER_DIM
    k0 = q0 + Q_RANK
    r0 = k0 + KV_RANK
    qn_ref[...] = _rms(p_ref[:, q0:k0], qw_ref[...]).astype(BF16)
    ckv_ref[...] = _rms(p_ref[:, k0:r0], kw_ref[...])
    kr_ref[...] = p_ref[:, r0:]


def _split_proj(proj, q_norm_w, kv_norm_w, layer, tm=512):
    return pl.pallas_call(
        _split_body,
        out_shape=[jax.ShapeDtypeStruct((N_TOK, Q_RANK), BF16),
                   jax.ShapeDtypeStruct((N_TOK, KV_RANK), F32),
                   jax.ShapeDtypeStruct((N_TOK, ROPE_DIM), F32)],
        grid=(N_TOK // tm,),
        in_specs=[pl.BlockSpec((tm, MIX_IN), lambda t: (t, 0)),
                  pl.BlockSpec((None, 1, Q_RANK), lambda t: (layer, 0, 0)),
                  pl.BlockSpec((None, 1, KV_RANK), lambda t: (layer, 0, 0))],
        out_specs=[pl.BlockSpec((tm, Q_RANK), lambda t: (t, 0)),
                   pl.BlockSpec((tm, KV_RANK), lambda t: (t, 0)),
                   pl.BlockSpec((tm, ROPE_DIM), lambda t: (t, 0))],
        compiler_params=_params(("arbitrary",)),
        name="split_proj",
    )(proj, q_norm_w.reshape(N_EVEN, 1, Q_RANK), kv_norm_w.reshape(N_EVEN, 1, KV_RANK))


def _bf16_pair(a):
    hi = a.astype(np.float32).astype(jnp.bfloat16)
    lo = (a - hi.astype(np.float64)).astype(np.float32).astype(jnp.bfloat16)
    return np.stack([hi, lo])


@functools.lru_cache(maxsize=None)
def _dft_tables(length):
    k = np.arange(length)
    ang = 2.0 * np.pi * (np.outer(k, k) % length) / length
    scale = 1.0 / np.sqrt(length * FOURIER_GROUP_DIM)
    t_len = np.concatenate([np.cos(ang), -np.sin(ang)], axis=1) * scale
    c = np.arange(FOURIER_GROUP_DIM)
    angc = 2.0 * np.pi * (np.outer(c, c) % FOURIER_GROUP_DIM) / FOURIER_GROUP_DIM
    eye = np.eye(FOURIER_GROUPS)
    t_ch = np.concatenate([np.kron(eye, np.cos(angc)), np.kron(eye, np.sin(angc))], axis=1)
    return _bf16_pair(t_len), _bf16_pair(t_ch)


def _fourier_body(f_ref, tl_ref, tc_ref, *rest):
    o_ref = rest[-1]
    gh = _dot3(f_ref[...], tc_ref[0], tc_ref[1])
    stacked = jnp.concatenate([gh[:, :FOURIER_DIM], gh[:, FOURIER_DIM:]], axis=0)
    s_hi, s_lo = _split2(stacked)
    y = _dot(tl_ref[0], s_hi) + _dot(tl_ref[1], s_hi) + _dot(tl_ref[0], s_lo)
    o_ref[...] = y.astype(BF16)


def _fourier(proj, n_seq, length, row0, prev=None):
    t_len, t_ch = _dft_tables(length)
    blk0 = row0 // length
    in_specs = [pl.BlockSpec((length, FOURIER_DIM), lambda b: (blk0 + b, 0)),
                pl.BlockSpec((2, length, 2 * length), lambda b: (0, 0, 0)),
                pl.BlockSpec((2, FOURIER_DIM, 2 * FOURIER_DIM), lambda b: (0, 0, 0))]
    args = [proj, jnp.asarray(t_len), jnp.asarray(t_ch)]
    aliases = {}
    if prev is not None:
        in_specs.append(ANY_SPEC)
        args.append(prev)
        aliases = {3: 0}
    return pl.pallas_call(
        _fourier_body,
        out_shape=jax.ShapeDtypeStruct((N_TOK, FOURIER_DIM), BF16),
        grid=(n_seq,),
        in_specs=in_specs,
        out_specs=pl.BlockSpec((length, FOURIER_DIM), lambda b: (blk0 + b, 0)),
        input_output_aliases=aliases,
        compiler_params=_params(("arbitrary",)),
        name="fourier",
    )(*args)


def _rope_pair(x, cos, sin_signed):
    lane = lax.broadcasted_iota(jnp.int32, x.shape, 1)
    first_half = (lane & 31) < 16
    partner = jnp.where(first_half, pltpu.roll(x, LANES - 16, 1), pltpu.roll(x, 16, 1))
    return x * cos + partner * sin_signed


def _attn_body(*refs, rope, ctx):
    q_ref, kv_ref, kr_ref = refs[:3]
    refs = refs[3:]
    if rope:
        cq_ref, sq_ref, ck_ref, sk_ref = refs[:4]
        refs = refs[4:]
    if ctx:
        kvc_ref, krc_ref = refs[:2]
        refs = refs[2:]
    o_ref = refs[-1]

    def half_lanes(x):
        lane = lax.broadcasted_iota(jnp.int32, x.shape, 1)
        return jnp.where(lane < ROPE_DIM, x, 0.0).astype(BF16), jnp.where(lane >= ROPE_DIM, x, 0.0).astype(BF16)

    kr = kr_ref[...]
    kr = jnp.concatenate([kr, kr], axis=1)
    if rope:
        kr = _rope_pair(kr, ck_ref[...], sk_ref[...])
    kr_halves = half_lanes(kr)
    if ctx:
        krc = krc_ref[...]
        krc_halves = half_lanes(jnp.concatenate([krc, krc], axis=1))
    rope0 = MLA_HEADS * NOPE_DIM

    def keys_values(ref, c0, kr_half):
        k = jnp.concatenate([ref[:, c0:c0 + NOPE_DIM], kr_half], axis=1)
        v = ref[:, c0 + NOPE_DIM:c0 + NOPE_DIM + V_DIM]
        return k, jnp.concatenate([v, jnp.ones(v.shape, BF16)], axis=1)

    for pr in range(MLA_HEADS // 2):
        qr = q_ref[:, rope0 + pr * LANES:rope0 + (pr + 1) * LANES]
        if rope:
            qr = _rope_pair(qr, cq_ref[...], sq_ref[...])
        qr_halves = half_lanes(qr * ATTN_SCALE)
        for hh in range(2):
            h = 2 * pr + hh
            qn = (q_ref[:, h * NOPE_DIM:(h + 1) * NOPE_DIM] * ATTN_SCALE).astype(BF16)
            qcat = jnp.concatenate([qn, qr_halves[hh]], axis=1)
            c0 = h * (NOPE_DIM + V_DIM)
            k, v1 = keys_values(kv_ref, c0, kr_halves[hh])
            s = _dot_nt(qcat, k)
            m = jnp.max(s, axis=-1, keepdims=True)
            if ctx:
                kc, vc1 = keys_values(kvc_ref, c0, krc_halves[hh])
                sc = _dot_nt(qcat, kc)
                m = jnp.maximum(m, jnp.max(sc, axis=-1, keepdims=True))
            acc = _dot(jnp.exp(s - m).astype(BF16), v1)
            if ctx:
                acc = acc + _dot(jnp.exp(sc - m).astype(BF16), vc1)
            o_ref[:, h * V_DIM:(h + 1) * V_DIM] = (acc[:, :V_DIM] / acc[:, V_DIM:V_DIM + 1]).astype(BF16)


def _attention(q, kv, kr, *, n_seq, length, row0, tq, rope_tabs=None, ctx=None, prev=None):
    n_qt = length // tq
    qblk0 = row0 // tq
    kblk0 = row0 // length
    kv_w = MLA_HEADS * (NOPE_DIM + V_DIM)
    in_specs = [pl.BlockSpec((tq, MLA_HEADS * QK_DIM), lambda b, i: (qblk0 + b * n_qt + i, 0)),
                pl.BlockSpec((length, kv_w), lambda b, i: (kblk0 + b, 0)),
                pl.BlockSpec((length, ROPE_DIM), lambda b, i: (kblk0 + b, 0))]
    args = [q, kv, kr]
    if rope_tabs is not None:
        cos, sin = rope_tabs
        in_specs += [pl.BlockSpec((tq, LANES), lambda b, i: (i, 0)),
                     pl.BlockSpec((tq, LANES), lambda b, i: (i, 0)),
                     pl.BlockSpec((length, LANES), lambda b, i: (0, 0)),
                     pl.BlockSpec((length, LANES), lambda b, i: (0, 0))]
        args += [cos, sin, cos, sin]
    if ctx is not None:
        krc, cblk0 = ctx
        in_specs += [pl.BlockSpec((PAST_LEN, kv_w), lambda b, i: (cblk0 + b, 0)),
                     pl.BlockSpec((None, PAST_LEN, ROPE_DIM), lambda b, i: (b, 0, 0))]
        args += [kv, krc]
    aliases = {}
    if prev is not None:
        aliases = {len(args): 0}
        in_specs.append(ANY_SPEC)
        args.append(prev)
    return pl.pallas_call(
        functools.partial(_attn_body, rope=rope_tabs is not None, ctx=ctx is not None),
        out_shape=jax.ShapeDtypeStruct((N_TOK, MLA_HEADS * V_DIM), BF16),
        grid=(n_seq, n_qt),
        in_specs=in_specs,
        out_specs=pl.BlockSpec((tq, MLA_HEADS * V_DIM), lambda b, i: (qblk0 + b * n_qt + i, 0)),
        input_output_aliases=aliases,
        compiler_params=_params(("arbitrary", "arbitrary")),
        name="attention",
    )(*args)


def _rope_tables():
    rows = DEC_SEQ // GRID_W
    row = jnp.repeat(jnp.arange(rows, dtype=F32), GRID_W)
    col = jnp.tile(jnp.arange(GRID_W, dtype=F32), rows)
    inv = ROPE_THETA ** (-jnp.arange(ROPE_DIM // 4, dtype=F32) * 2.0 / (ROPE_DIM // 2))
    ang = jnp.stack([row[:, None] * inv, col[:, None] * inv], axis=1)
    cos, sin = jnp.cos(ang), jnp.sin(ang)
    cos64 = jnp.stack([cos, cos], axis=2).reshape(DEC_SEQ, ROPE_DIM)
    sin64 = jnp.stack([-sin, sin], axis=2).reshape(DEC_SEQ, ROPE_DIM)
    return jnp.tile(cos64, (1, 2)), jnp.tile(sin64, (1, 2))


_CONV_PAD = 8
_DIR_HEADS = 2 * HEADS_PER_GROUP


@functools.lru_cache(maxsize=None)
def _head_expand():
    e = np.zeros((_DIR_HEADS, 2 * GROUP_DIM), np.float32)
    for h in range(HEADS_PER_GROUP):
        e[h, h * SSM_HEAD_DIM:(h + 1) * SSM_HEAD_DIM] = 1.0
        e[HEADS_PER_GROUP + h, GROUP_DIM + h * SSM_HEAD_DIM:GROUP_DIM + (h + 1) * SSM_HEAD_DIM] = 1.0
    return e.astype(jnp.bfloat16)


def _scan_body(*refs, length, zero_init, emit_state, n_prev):
    (xr_ref, br_ref, cr_ref, wx_ref, wb_ref, wc_ref, bx_ref, bb_ref, bc_ref,
     dt_ref, bias_ref, acol_ref, dskip_ref, e_ref) = refs[:14]
    refs = refs[14:]
    if not zero_init:
        h0f_ref, h0b_ref = refs[:2]
        refs = refs[2:]
    refs = refs[n_prev:]
    y_ref = refs[0]
    refs = refs[1:]
    if emit_state:
        hf_ref, hb_ref = refs[:2]
        refs = refs[2:]
    padx, padb, padc, y_scr, bm_scr, cm_scr, xwf_scr, xwb_scr, decf_scr, decb_scr, tt_scr = refs
    q = SSM_CHUNK
    n_chunks = length // q
    hpg = HEADS_PER_GROUP
    head0 = pl.multiple_of(pl.program_id(1) * hpg, hpg)
    hd = SSM_HEAD_DIM
    state_shape = (GROUP_DIM, SSM_STATE)

    for raw, pad in ((xr_ref, padx), (br_ref, padb), (cr_ref, padc)):
        zeros = jnp.zeros((_CONV_PAD, pad.shape[1]), F32)
        pad[0:_CONV_PAD, :] = zeros
        pad[_CONV_PAD + length:, :] = zeros
        pad[_CONV_PAD:_CONV_PAD + length, :] = raw[...].astype(F32)

    rr = lax.broadcasted_iota(jnp.int32, (q, q), 0)
    cc = lax.broadcasted_iota(jnp.int32, (q, q), 1)
    lower = cc <= rr
    upper = cc >= rr
    tri_t = upper.astype(BF16)
    pair_lane = lax.broadcasted_iota(jnp.int32, (q, 2 * hd), 1)
    fwd_rows = lax.broadcasted_iota(jnp.int32, (_DIR_HEADS, q), 0) < hpg
    pad_rows = jnp.zeros((q - _DIR_HEADS, q), F32)
    a_neg = -jnp.exp(acol_ref[...])
    bias = bias_ref[...]
    e16 = e_ref[...]

    def conv_silu(pad, w_ref, b_ref, r0):
        n_win = q + 2 * _CONV_PAD
        win = pad[r0:r0 + n_win, :]
        acc = b_ref[...]
        for k in range(SSM_CONV):
            off = _CONV_PAD + k - SSM_CONV // 2
            tap = win[off:off + q, :] if off == _CONV_PAD else pltpu.roll(win, n_win - off, 0)[0:q, :]
            acc = acc + w_ref[k:k + 1, :] * tap
        return _silu(acc)

    totals = []
    for c in range(n_chunks):
        r0 = c * q
        rows = slice(r0, r0 + q)
        x = conv_silu(padx, wx_ref, bx_ref, r0)
        bm = conv_silu(padb, wb_ref, bb_ref, r0).astype(BF16)
        cm = conv_silu(padc, wc_ref, bc_ref, r0).astype(BF16)
        bm_scr[rows, :] = bm
        cm_scr[rows, :] = cm
        tt_scr[c] = (dt_ref[rows, :] + bias).T
        t = jnp.concatenate([tt_scr[c, pl.ds(head0, hpg), :], tt_scr[c, pl.ds(SSM_HEADS + head0, hpg), :]],
                            axis=0)
        dt_t = jnp.maximum(t, 0.0) + jnp.log1p(jnp.exp(-jnp.abs(t)))
        a_t = dt_t * a_neg
        a_hi = a_t.astype(BF16)
        r1 = a_t - a_hi.astype(F32)
        a_mid = r1.astype(BF16)
        a_lo = (r1 - a_mid.astype(F32)).astype(BF16)
        cs_t = _dot(a_hi, tri_t) + _dot(a_mid, tri_t) + _dot(a_lo, tri_t)
        ecs_t = cs_t - a_t
        total = cs_t[:, q - 1:q]
        totals.append(jnp.exp(total))
        cs = jnp.concatenate([cs_t, pad_rows], axis=0).T
        ecs = jnp.concatenate([ecs_t, pad_rows], axis=0).T
        cb = _dot_nt(cm, bm)
        pieces = []
        for pr in range(hpg // 2):
            mats = []
            for h in (2 * pr, 2 * pr + 1):
                lf = jnp.exp(jnp.where(lower, cs[:, h:h + 1] - cs_t[h:h + 1, :], NEG_BIG))
                lb = jnp.exp(jnp.where(upper, ecs_t[hpg + h:hpg + h + 1, :] - ecs[:, hpg + h:hpg + h + 1], NEG_BIG))
                mats.append((cb * (lf * dt_t[h:h + 1, :] + lb * dt_t[hpg + h:hpg + h + 1, :])).astype(BF16))
            xp = x[:, pr * 2 * hd:(pr + 1) * 2 * hd]
            rhs = jnp.concatenate([jnp.where(pair_lane < hd, xp, 0.0), jnp.where(pair_lane >= hd, xp, 0.0)], axis=0)
            pieces.append(_dot(jnp.concatenate(mats, axis=1), rhs.astype(BF16)))
        y_scr[rows, :] = jnp.concatenate(pieces, axis=1) + dskip_ref[...] * x
        dec = _dot_tn(jnp.exp(jnp.where(fwd_rows, cs_t, total - ecs_t)).astype(BF16), e16)
        wgt = _dot_tn((jnp.exp(jnp.where(fwd_rows, total - cs_t, ecs_t)) * dt_t).astype(BF16), e16)
        decf_scr[rows, :] = dec[:, :GROUP_DIM].astype(BF16)
        decb_scr[rows, :] = dec[:, GROUP_DIM:].astype(BF16)
        xwf_scr[rows, :] = (x * wgt[:, :GROUP_DIM]).astype(BF16)
        xwb_scr[rows, :] = (x * wgt[:, GROUP_DIM:]).astype(BF16)

    def step(state, c, dec_scr, xw_scr, row0):
        rows = slice(c * q, (c + 1) * q)
        y_off = _dot_nt(cm_scr[rows, :], state.astype(BF16))
        y_scr[rows, :] = y_scr[rows, :] + y_off * dec_scr[rows, :].astype(F32)
        upd = _dot_tn(xw_scr[rows, :], bm_scr[rows, :])
        decay = totals[c]
        return jnp.concatenate(
            [state[h * hd:(h + 1) * hd, :] * decay[row0 + h:row0 + h + 1, :] + upd[h * hd:(h + 1) * hd, :]
             for h in range(hpg)], axis=0)

    if zero_init:
        hf = jnp.zeros(state_shape, F32)
        hb = jnp.zeros(state_shape, F32)
    else:
        hf = h0f_ref[...].reshape(state_shape)
        hb = h0b_ref[...].reshape(state_shape)
    for i in range(n_chunks):
        hf = step(hf, i, decf_scr, xwf_scr, 0)
        hb = step(hb, n_chunks - 1 - i, decb_scr, xwb_scr, hpg)
    y_ref[...] = y_scr[...].astype(BF16)
    if emit_state:
        hf_ref[...] = hf.reshape(hf_ref.shape)
        hb_ref[...] = hb.reshape(hb_ref.shape)


def _scan(xbc, dt, conv_w, conv_b, dt_bias, alog_col, dskip, layer, *, n_seq, length, row0, h0=None,
          y_prev=None, state_prev=None, emit_state):
    blk0 = row0 // length
    b0 = SSM_INNER // SSM_STATE
    c0 = b0 + SSM_GROUPS
    seq = lambda width, col: pl.BlockSpec((length, width), lambda b, g: (blk0 + b, col + g))
    cw = lambda width, col: pl.BlockSpec((None, SSM_CONV, width), lambda b, g: (layer, 0, col + g))
    cbias = lambda width, col: pl.BlockSpec((None, 1, width), lambda b, g: (layer, 0, col + g))
    grp = lambda width: pl.BlockSpec((None, 1, width), lambda b, g: (layer, 0, g))
    h0_spec = pl.BlockSpec((None, None, HEADS_PER_GROUP, SSM_HEAD_DIM, SSM_STATE), lambda b, g: (b, layer, g, 0, 0))
    in_specs = [seq(GROUP_DIM, 0), seq(SSM_STATE, b0), seq(SSM_STATE, c0),
                cw(GROUP_DIM, 0), cw(SSM_STATE, b0), cw(SSM_STATE, c0),
                cbias(GROUP_DIM, 0), cbias(SSM_STATE, b0), cbias(SSM_STATE, c0),
                pl.BlockSpec((length, 2 * SSM_HEADS), lambda b, g: (blk0 + b, 0)),
                pl.BlockSpec((None, 1, 2 * SSM_HEADS), lambda b, g: (layer, 0, 0)),
                pl.BlockSpec((None, None, _DIR_HEADS, 1), lambda b, g: (layer, g, 0, 0)), grp(GROUP_DIM),
                pl.BlockSpec((_DIR_HEADS, 2 * GROUP_DIM), lambda b, g: (0, 0))]
    cb3 = conv_b.reshape(N_ODD, 1, SSM_CONV_DIM)
    args = [xbc, xbc, xbc, conv_w, conv_w, conv_w, cb3, cb3, cb3, dt, dt_bias, alog_col, dskip,
            jnp.asarray(_head_expand())]
    if h0 is not None:
        in_specs += [h0_spec, h0_spec]
        args += list(h0)
    out_shape = [jax.ShapeDtypeStruct((N_TOK, SSM_INNER), BF16)]
    out_specs = [seq(GROUP_DIM, 0)]
    if emit_state:
        st = jax.ShapeDtypeStruct((n_seq, N_ODD, SSM_HEADS, SSM_HEAD_DIM, SSM_STATE), F32)
        out_shape += [st, st]
        out_specs += [h0_spec, h0_spec]
    aliases = {}
    prevs = ([] if y_prev is None else [y_prev]) + ([] if state_prev is None else list(state_prev))
    for k, prev in enumerate(prevs):
        aliases[len(args)] = k if y_prev is not None else k + 1
        in_specs.append(ANY_SPEC)
        args.append(prev)
    pad_rows = length + 2 * _CONV_PAD
    scratch = [pltpu.VMEM((pad_rows, GROUP_DIM), F32), pltpu.VMEM((pad_rows, SSM_STATE), F32),
               pltpu.VMEM((pad_rows, SSM_STATE), F32), pltpu.VMEM((length, GROUP_DIM), F32),
               pltpu.VMEM((length, SSM_STATE), BF16), pltpu.VMEM((length, SSM_STATE), BF16),
               pltpu.VMEM((length, GROUP_DIM), BF16), pltpu.VMEM((length, GROUP_DIM), BF16),
               pltpu.VMEM((length, GROUP_DIM), BF16), pltpu.VMEM((length, GROUP_DIM), BF16),
               pltpu.VMEM((length // SSM_CHUNK, 2 * SSM_HEADS, SSM_CHUNK), F32)]
    return pl.pallas_call(
        functools.partial(_scan_body, length=length, zero_init=h0 is None, emit_state=emit_state,
                          n_prev=len(prevs)),
        out_shape=out_shape,
        grid=(n_seq, SSM_GROUPS),
        in_specs=in_specs,
        out_specs=out_specs,
        scratch_shapes=scratch,
        input_output_aliases=aliases,
        compiler_params=_params(("arbitrary", "arbitrary")),
        name="ssd_scan",
    )(*args)


def _gate_norm_body(y_ref, z_ref, w_ref, o_ref):
    g = y_ref[...].astype(F32) * _silu(z_ref[...].astype(F32))
    o_ref[...] = _rms(g, w_ref[...]).astype(BF16)


def _gate_norm(y, z, norm_w, layer, tm=512):
    return pl.pallas_call(
        _gate_norm_body,
        out_shape=jax.ShapeDtypeStruct((N_TOK, SSM_INNER), BF16),
        grid=(N_TOK // tm,),
        in_specs=[pl.BlockSpec((tm, SSM_INNER), lambda t: (t, 0)),
                  pl.BlockSpec((tm, SSM_INNER), lambda t: (t, 0)),
                  pl.BlockSpec((None, 1, SSM_INNER), lambda t: (layer, 0, 0))],
        out_specs=pl.BlockSpec((tm, SSM_INNER), lambda t: (t, 0)),
        compiler_params=_params(("arbitrary",)),
        name="gate_norm",
    )(y, z, norm_w.reshape(N_ODD, 1, SSM_INNER))


def _moe_up_body(e_ref, n_ref, m_ref, first_ref, valid_ref, x_ref, wg_ref, wu_ref, o_ref, gb, ub):
    s = pl.program_id(0)

    def emit(rows, wg, wu):
        x = x_ref[rows, :]
        o_ref[rows, :] = (_silu(_dot(x, wg)) * _dot(x, wu)).astype(BF16)

    @pl.when(first_ref[s] == 1)
    def _():
        wg = wg_ref[...].astype(BF16)
        wu = wu_ref[...].astype(BF16)
        gb[...] = wg
        ub[...] = wu
        emit(slice(None), wg, wu)

    @pl.when((first_ref[s] == 0) & (valid_ref[s] == _TILE_FULL))
    def _():
        emit(slice(None), gb[...], ub[...])

    @pl.when((first_ref[s] == 0) & (valid_ref[s] == _TILE_HALF))
    def _():
        emit(slice(0, MOE_TM // 2), gb[...], ub[...])


def _moe_down_body(e_ref, n_ref, m_ref, first_ref, valid_ref, h_ref, w_ref, o_ref, wb):
    s = pl.program_id(0)

    def emit(rows, w):
        o_ref[rows, :] = _dot(h_ref[rows, :], w).astype(BF16)

    @pl.when(first_ref[s] == 1)
    def _():
        w = w_ref[...].astype(BF16)
        wb[...] = w
        emit(slice(None), w)

    @pl.when((first_ref[s] == 0) & (valid_ref[s] == _TILE_FULL))
    def _():
        emit(slice(None), wb[...])

    @pl.when((first_ref[s] == 0) & (valid_ref[s] == _TILE_HALF))
    def _():
        emit(slice(0, MOE_TM // 2), wb[...])


_TILE_SKIP, _TILE_FULL, _TILE_HALF = 0, 1, 2


def _moe_tables(tiles, count, n_col_tiles):
    n_steps = MOE_TILES * n_col_tiles
    tile_start = jnp.cumsum(tiles) - tiles
    step_end = jnp.cumsum(tiles * n_col_tiles)
    total = step_end[-1]
    s = jnp.minimum(jnp.arange(n_steps, dtype=jnp.int32), total - 1)
    e = jnp.sum((s[:, None] >= step_end[None, :]).astype(jnp.int32), axis=1)
    local = s - (step_end - tiles * n_col_tiles)[e]
    te = jnp.maximum(tiles[e], 1)
    n = local // te
    k = local - n * te
    m = tile_start[e] + k
    live = jnp.arange(n_steps) < total
    half = (count[e] - k * MOE_TM) <= MOE_TM // 2
    kind = jnp.where(live, jnp.where(half, _TILE_HALF, _TILE_FULL), _TILE_SKIP).astype(jnp.int32)
    first = ((k == 0) & live).astype(jnp.int32)
    return e.astype(jnp.int32), n.astype(jnp.int32), m.astype(jnp.int32), first, kind


def _moe(u, idx, w_gate, w_up, w_down, layer):
    tm = MOE_TM
    flat_e = idx.reshape(-1)
    onehot = (flat_e[:, None] == jnp.arange(N_EXPERTS)[None, :]).astype(jnp.int32)
    rank = jnp.sum((jnp.cumsum(onehot, axis=0) - onehot) * onehot, axis=1)
    count = jnp.sum(onehot, axis=0)
    tiles = (count + tm - 1) // tm
    row_start = (jnp.cumsum(tiles) - tiles) * tm
    dest = row_start[flat_e] + rank
    token = jnp.arange(N_TOK * TOP_K, dtype=jnp.int32) // TOP_K
    src = jnp.zeros((MOE_ROWS,), jnp.int32).at[dest].set(token, mode="promise_in_bounds", unique_indices=True)
    x_sorted = u.at[src].get(mode="promise_in_bounds")

    tn = 1024
    n_up = EXPERT_DIM // tn
    tabs = _moe_tables(tiles, count, n_up)
    w_spec = pl.BlockSpec((None, None, D_MODEL, tn), lambda s, e, n, m, f, v: (layer, e[s], 0, n[s]))
    hmid = pl.pallas_call(
        _moe_up_body,
        out_shape=jax.ShapeDtypeStruct((MOE_ROWS, EXPERT_DIM), BF16),
        grid_spec=pltpu.PrefetchScalarGridSpec(
            num_scalar_prefetch=5,
            grid=(MOE_TILES * n_up,),
            in_specs=[pl.BlockSpec((tm, D_MODEL), lambda s, e, n, m, f, v: (m[s], 0)), w_spec, w_spec],
            out_specs=pl.BlockSpec((tm, tn), lambda s, e, n, m, f, v: (m[s], n[s])),
            scratch_shapes=[pltpu.VMEM((D_MODEL, tn), BF16), pltpu.VMEM((D_MODEL, tn), BF16)]),
        compiler_params=_params(("arbitrary",)),
        name="moe_up",
    )(*tabs, x_sorted, w_gate, w_up)

    tn = 1024
    n_dn = D_MODEL // tn
    tabs = _moe_tables(tiles, count, n_dn)
    y_sorted = pl.pallas_call(
        _moe_down_body,
        out_shape=jax.ShapeDtypeStruct((MOE_ROWS, D_MODEL), BF16),
        grid_spec=pltpu.PrefetchScalarGridSpec(
            num_scalar_prefetch=5,
            grid=(MOE_TILES * n_dn,),
            in_specs=[pl.BlockSpec((tm, EXPERT_DIM), lambda s, e, n, m, f, v: (m[s], 0)),
                      pl.BlockSpec((None, None, EXPERT_DIM, tn), lambda s, e, n, m, f, v: (layer, e[s], 0, n[s]))],
            out_specs=pl.BlockSpec((tm, tn), lambda s, e, n, m, f, v: (m[s], n[s])),
            scratch_shapes=[pltpu.VMEM((EXPERT_DIM, tn), BF16)]),
        compiler_params=_params(("arbitrary",), vmem_limit=56 * 1024 * 1024),
        name="moe_down",
    )(*tabs, hmid, w_down)

    dest2 = dest.reshape(N_TOK, TOP_K)
    return (y_sorted.at[dest2[:, 0]].get(mode="promise_in_bounds"),
            y_sorted.at[dest2[:, 1]].get(mode="promise_in_bounds"))


def _q_perm():
    h = np.arange(MLA_HEADS)[:, None]
    nope = (h * QK_DIM + np.arange(NOPE_DIM)[None, :]).reshape(-1)
    rope = (h * QK_DIM + NOPE_DIM + np.arange(ROPE_DIM)[None, :]).reshape(-1)
    return np.concatenate([nope, rope])


def _even_mixer(u, j, p, cache_ckv, cache_krope, rope_tabs):
    proj = _matmul([u], [p['mix_w_in']], layer=j, n_cols=MIX_IN, tn=MIX_IN, tm=512, out_dtype=F32, name="mix_in")
    qn, ckv, kr = _split_proj(proj, p['q_norm_w'], p['kv_norm_w'], j)
    f = _fourier(proj, BATCH, SEQ, 0)
    f = _fourier(proj, DEC_BATCH, DEC_SEQ, N_TOK_P, prev=f)
    q = _matmul([qn], [p['w_q_perm']], layer=j, n_cols=MLA_HEADS * QK_DIM, tn=MLA_HEADS * QK_DIM, tm=1024,
                out_dtype=F32, name="q_proj")
    ckv_all = jnp.concatenate([ckv, cache_ckv[:, j].reshape(DEC_BATCH * PAST_LEN, KV_RANK)], axis=0).astype(BF16)
    kv = _matmul([ckv_all], [p['w_kv_b']], layer=j, n_cols=MLA_HEADS * (NOPE_DIM + V_DIM),
                 tn=MLA_HEADS * (NOPE_DIM + V_DIM), tm=1024, out_dtype=BF16, name="kv_proj")
    o = _attention(q, kv, kr, n_seq=BATCH, length=SEQ, row0=0, tq=SEQ)
    o = _attention(q, kv, kr, n_seq=DEC_BATCH, length=DEC_SEQ, row0=N_TOK_P, tq=256, rope_tabs=rope_tabs,
                   ctx=(cache_krope[:, j], N_TOK // PAST_LEN), prev=o)
    h = _matmul([f, o], [p['mix_w_out']], layer=j, n_cols=D_MODEL, tn=1024, tm=1024, out_dtype=BF16, name="mix_out")
    return h, ckv, kr


def _group_head_rows(w):
    w = w.reshape(N_ODD, 2, SSM_GROUPS, HEADS_PER_GROUP)
    return jnp.moveaxis(w, 1, 2).reshape(N_ODD, SSM_GROUPS, _DIR_HEADS, 1)


def _odd_mixer(u, j, p, state_f, state_b, state_prev):
    z = _matmul([u], [p['ssm_w_in']], layer=j, n_cols=SSM_INNER, tn=1024, tm=1024, out_dtype=BF16, name="ssm_in_z")
    xbc = _matmul([u], [p['ssm_w_in']], layer=j, n_cols=SSM_CONV_DIM, col0=SSM_INNER, tn=1024, tm=1024,
                  out_dtype=BF16, name="ssm_in_xbc")
    dt = _matmul([u], [p['ssm_w_in']], layer=j, n_cols=2 * SSM_HEADS, col0=SSM_INNER + SSM_CONV_DIM, tn=2 * SSM_HEADS,
                 tm=1024, out_dtype=F32, name="ssm_dt")
    common = (xbc, dt, p['ssm_conv_w'], p['ssm_conv_b'], p['ssm_dt_bias'], p['ssm_alog_col'], p['ssm_dskip'], j)
    y, hf, hb = _scan(*common, n_seq=BATCH, length=SEQ, row0=0, state_prev=state_prev, emit_state=True)
    (y,) = _scan(*common, n_seq=DEC_BATCH, length=DEC_SEQ, row0=N_TOK_P, h0=(state_f, state_b), y_prev=y,
                 emit_state=False)
    yn = _gate_norm(y, z, p['ssm_norm_w'], j)
    h = _matmul([yn], [p['ssm_w_out']], layer=j, n_cols=D_MODEL, tn=512, tm=1024, out_dtype=BF16, name="ssm_out")
    return h, hf, hb


def kernel(x_prompt, x_sample, cache_ckv, cache_krope, state_ssm_fwd, state_ssm_bwd, c, c_ctx, ada_w, ada_b, ln1_g, ln1_b, ln2_g, ln2_b, mix_w_in, q_norm_w, w_q_b, kv_norm_w, w_kv_b, mix_w_out, ffn_w_gate, ffn_w_up, ffn_w_down, ssm_w_in, ssm_conv_w, ssm_conv_b, ssm_dt_bias, ssm_a_log, ssm_d, ssm_norm_w, ssm_w_out, moe_router, moe_w_gate, moe_w_up, moe_w_down):
    p = dict(mix_w_in=mix_w_in, q_norm_w=q_norm_w, w_q_perm=w_q_b[:, :, _q_perm()], kv_norm_w=kv_norm_w,
             w_kv_b=w_kv_b, mix_w_out=mix_w_out, ssm_w_in=ssm_w_in, ssm_conv_w=ssm_conv_w, ssm_conv_b=ssm_conv_b,
             ssm_dt_bias=ssm_dt_bias.reshape(N_ODD, 1, 2 * SSM_HEADS),
             ssm_alog_col=_group_head_rows(ssm_a_log),
             ssm_dskip=jnp.repeat(ssm_d, SSM_HEAD_DIM, axis=1).reshape(N_ODD, 1, SSM_INNER),
             ssm_norm_w=ssm_norm_w, ssm_w_out=ssm_w_out)
    cond = jnp.concatenate([c_ctx[None, :], c, jnp.zeros((COND_ROWS - N_COND, D_MODEL), F32)], axis=0)
    mod = _ada(cond, ada_w, ada_b)
    mod5 = mod[:, :N_COND].reshape(DEPTH, N_COND, N_MOD, 1, D_MODEL)
    rope_tabs = _rope_tables()

    x, u = _modulate(x_prompt, x_sample, mod5, 0)
    ckvs, krs = [], []
    states = None
    for i in range(DEPTH):
        j = i // 2
        if i % 2 == 0:
            h, ckv, kr = _even_mixer(u, j, p, cache_ckv, cache_krope, rope_tabs)
            ckvs.append(ckv[:N_TOK_P].reshape(BATCH, SEQ, KV_RANK))
            krs.append(kr[:N_TOK_P].reshape(BATCH, SEQ, ROPE_DIM))
            x, u = _ln(x, h, mod5, ln1_g, ln1_b, layer=i, gate_chunk=2, mod_next=(i, 4, 3))
            hmid = _matmul([u], [ffn_w_gate, ffn_w_up], layer=j, n_cols=FFN_DIM, tn=512, tm=1024, out_dtype=BF16,
                           swiglu=True, name="ffn_up")
            f = _matmul([hmid], [ffn_w_down], layer=j, n_cols=D_MODEL, tn=512, tm=512, out_dtype=BF16,
                        name="ffn_down")
            pair_probs = None
        else:
            h, hf, hb = _odd_mixer(u, j, p, state_ssm_fwd, state_ssm_bwd, states)
            states = (hf, hb)
            router = jnp.pad(moe_router[j], ((0, 0), (0, LANES - N_EXPERTS)))
            x, u, pair_probs, idx = _ln(x, h, mod5, ln1_g, ln1_b, layer=i, gate_chunk=2, mod_next=(i, 4, 3),
                                        router=router)
            f = _moe(u, idx[:, :TOP_K], moe_w_gate, moe_w_up, moe_w_down, j)
        if i + 1 < DEPTH:
            x, u = _ln(x, f, mod5, ln2_g, ln2_b, layer=i, gate_chunk=5, mod_next=(i + 1, 1, 0),
                       pair_probs=pair_probs)
        else:
            y_prompt, y_sample = _ln(x, f, mod5, ln2_g, ln2_b, layer=i, gate_chunk=5, pair_probs=pair_probs)
    return (y_prompt.reshape(BATCH, SEQ, D_MODEL), y_sample.reshape(DEC_BATCH, DEC_SEQ, D_MODEL),
            jnp.stack(ckvs, axis=1), jnp.stack(krs, axis=1), states[0], states[1])
```

```python
import functools

import numpy as np
import jax
import jax.numpy as jnp
from jax import lax
from jax.experimental import pallas as pl
from jax.experimental.pallas import tpu as pltpu

D_MODEL = 2048
BATCH = 16
SEQ = 256
DEPTH = 4
DEC_BATCH = 2
DEC_SEQ = 1024
PAST_LEN = 512
GRID_W = 64
N_EVEN = (DEPTH + 1) // 2
N_ODD = DEPTH // 2
FOURIER_GROUPS = 4
FOURIER_GROUP_DIM = 128
FOURIER_DIM = FOURIER_GROUPS * FOURIER_GROUP_DIM
MLA_HEADS = 12
Q_RANK = 768
KV_RANK = 256
NOPE_DIM = 128
ROPE_DIM = 64
V_DIM = 128
QK_DIM = NOPE_DIM + ROPE_DIM
ATTN_SCALE = QK_DIM ** -0.5
ROPE_THETA = 10000.0
MIX_IN = FOURIER_DIM + Q_RANK + KV_RANK + ROPE_DIM
SSM_INNER = 2 * D_MODEL
SSM_HEAD_DIM = 64
SSM_HEADS = SSM_INNER // SSM_HEAD_DIM
SSM_GROUPS = 8
SSM_STATE = 128
SSM_CONV = 5
SSM_CHUNK = 128
SSM_CONV_DIM = SSM_INNER + 2 * SSM_GROUPS * SSM_STATE
FFN_DIM = 5632
N_EXPERTS = 8
TOP_K = 2
EXPERT_DIM = 4096
ALPHA = (2 * DEPTH) ** 0.25
EPS = 1e-5

F32 = jnp.float32
BF16 = jnp.bfloat16

N_TOK_P = BATCH * SEQ
N_TOK_S = DEC_BATCH * DEC_SEQ
N_TOK = N_TOK_P + N_TOK_S
N_MOD = 6
N_COND = 1 + DEC_BATCH
COND_ROWS = 8
HEADS_PER_GROUP = SSM_HEADS // SSM_GROUPS
GROUP_DIM = HEADS_PER_GROUP * SSM_HEAD_DIM
LANES = 128
V7X_VMEM_LIMIT = 52 * 1024 * 1024
V7X_VMEM_LIMIT_LARGE = 56 * 1024 * 1024
NEG_BIG = -1e30

MOE_TM = 512
MOE_TILES = (N_TOK * TOP_K) // MOE_TM + N_EXPERTS
MOE_ROWS = MOE_TILES * MOE_TM

ANY_SPEC = pl.BlockSpec(memory_space=pl.ANY)


def _params(sem, vmem_limit=V7X_VMEM_LIMIT):
    return pltpu.CompilerParams(dimension_semantics=sem, vmem_limit_bytes=vmem_limit)


def _silu(x):
    return x * jax.nn.sigmoid(x)


def _split2(x):
    hi = x.astype(BF16)
    lo = (x - hi.astype(F32)).astype(BF16)
    return hi, lo


def _dot(a, b):
    return jnp.dot(a, b, preferred_element_type=F32)


def _dot_nt(a, b):
    return lax.dot_general(a, b, (((1,), (1,)), ((), ())), preferred_element_type=F32)


def _dot_tn(a, b):
    return lax.dot_general(a, b, (((0,), (0,)), ((), ())), preferred_element_type=F32)


def _dot3(a, b_hi, b_lo):
    a_hi, a_lo = _split2(a)
    return _dot(a_hi, b_hi) + _dot(a_lo, b_hi) + _dot(a_hi, b_lo)


def _cond_of_tile(t, tm):
    p_tiles = N_TOK_P // tm
    per_seq = DEC_SEQ // tm
    return jnp.where(t < p_tiles, 0, 1 + (t - p_tiles) // per_seq)


def _mod_spec(layer, chunk, tm):
    return pl.BlockSpec((None, None, None, 1, D_MODEL),
                        lambda t: (layer, _cond_of_tile(t, tm), chunk, 0, 0))


def _ada_body(c_ref, w_ref, b_ref, o_ref):
    s = _silu(c_ref[...]).astype(BF16)
    o_ref[...] = _dot(s, w_ref[...].astype(BF16)) + b_ref[...]


def _ada(cond, ada_w, ada_b):
    tn = 1024
    n_out = N_MOD * D_MODEL
    return pl.pallas_call(
        _ada_body,
        out_shape=jax.ShapeDtypeStruct((DEPTH, COND_ROWS, n_out), F32),
        grid=(DEPTH, n_out // tn),
        in_specs=[pl.BlockSpec((COND_ROWS, D_MODEL), lambda i, n: (0, 0)),
                  pl.BlockSpec((None, D_MODEL, tn), lambda i, n: (i, 0, n)),
                  pl.BlockSpec((None, 1, tn), lambda i, n: (i, 0, n))],
        out_specs=pl.BlockSpec((None, COND_ROWS, tn), lambda i, n: (i, 0, n)),
        compiler_params=_params(("arbitrary", "arbitrary")),
        name="ada",
    )(cond, ada_w, ada_b.reshape(DEPTH, 1, n_out))


def _mm_body(*refs, n_x, n_w, swiglu):
    x_refs = refs[:n_x]
    w_refs = refs[n_x:n_x + n_w]
    o_ref = refs[n_x + n_w]
    wb_refs = refs[n_x + n_w + 1:]

    def emit(ws):
        if n_x == 1:
            x = x_refs[0][...]
        else:
            x = jnp.concatenate([r[...] for r in x_refs], axis=1)
        accs = [_dot(x, w) for w in ws]
        out = _silu(accs[0]) * accs[1] if swiglu else accs[0]
        o_ref[...] = out.astype(o_ref.dtype)

    @pl.when(pl.program_id(1) == 0)
    def _():
        ws = [w_ref[...].astype(BF16) for w_ref in w_refs]
        for wb, w in zip(wb_refs, ws):
            wb[...] = w
        emit(ws)

    @pl.when(pl.program_id(1) != 0)
    def _():
        emit([wb[...] for wb in wb_refs])


def _matmul(xs, ws, *, layer, n_cols, tn, tm, out_dtype, col0=0, swiglu=False, name):
    m = xs[0].shape[0]
    k = sum(x.shape[1] for x in xs)
    assert m % tm == 0 and n_cols % tn == 0 and col0 % tn == 0 and all(w.shape[1] == k for w in ws)
    cb0 = col0 // tn
    in_specs = [pl.BlockSpec((tm, x.shape[1]), lambda n, r: (r, 0)) for x in xs]
    in_specs += [pl.BlockSpec((None, k, tn), lambda n, r: (layer, 0, cb0 + n)) for _ in ws]
    return pl.pallas_call(
        functools.partial(_mm_body, n_x=len(xs), n_w=len(ws), swiglu=swiglu),
        out_shape=jax.ShapeDtypeStruct((m, n_cols), out_dtype),
        grid=(n_cols // tn, m // tm),
        in_specs=in_specs,
        out_specs=pl.BlockSpec((tm, tn), lambda n, r: (r, n)),
        scratch_shapes=[pltpu.VMEM((k, tn), BF16) for _ in ws],
        compiler_params=_params(("arbitrary", "arbitrary")),
        name=name,
    )(*xs, *ws)


_LN_TM = 512
_P_TILES = N_TOK_P // _LN_TM


def _modulate_body(xp_ref, xs_ref, sc_ref, sh_ref, x_ref, u_ref):
    def emit(x):
        x_ref[...] = x
        u_ref[...] = (x * (1.0 + sc_ref[...]) + sh_ref[...]).astype(BF16)

    @pl.when(pl.program_id(0) < _P_TILES)
    def _():
        emit(xp_ref[...])

    @pl.when(pl.program_id(0) >= _P_TILES)
    def _():
        emit(xs_ref[...])


def _modulate(x_prompt, x_sample, mod5, layer):
    tm = _LN_TM
    row = pl.BlockSpec((tm, D_MODEL), lambda t: (t, 0))
    return pl.pallas_call(
        _modulate_body,
        out_shape=[jax.ShapeDtypeStruct((N_TOK, D_MODEL), F32), jax.ShapeDtypeStruct((N_TOK, D_MODEL), BF16)],
        grid=(N_TOK // tm,),
        in_specs=[pl.BlockSpec((tm, D_MODEL), lambda t: (jnp.minimum(t, _P_TILES - 1), 0)),
                  pl.BlockSpec((tm, D_MODEL), lambda t: (jnp.maximum(t - _P_TILES, 0), 0)),
                  _mod_spec(layer, 1, tm), _mod_spec(layer, 0, tm)],
        out_specs=[row, row],
        compiler_params=_params(("arbitrary",)),
        name="modulate",
    )(x_prompt.reshape(N_TOK_P, D_MODEL), x_sample.reshape(N_TOK_S, D_MODEL), mod5, mod5)


def _ln_body(*refs, modulate, route, gated_pair):
    if gated_pair:
        x_ref, h_ref, h2_ref, pr_ref, g_ref, lg_ref, lb_ref = refs[:7]
        refs = refs[7:]
        pr = pr_ref[...]
        h = h_ref[...].astype(F32) * pr[:, 0:1] + h2_ref[...].astype(F32) * pr[:, 1:2]
    else:
        x_ref, h_ref, g_ref, lg_ref, lb_ref = refs[:5]
        refs = refs[5:]
        h = h_ref[...].astype(F32)
    v = ALPHA * x_ref[...] + g_ref[...] * h
    mu = jnp.mean(v, axis=-1, keepdims=True)
    d = v - mu
    var = jnp.mean(d * d, axis=-1, keepdims=True)
    xn = d * lax.rsqrt(var + EPS) * lg_ref[...] + lb_ref[...]
    if not modulate:
        yp_ref, ys_ref = refs

        @pl.when(pl.program_id(0) < _P_TILES)
        def _():
            yp_ref[...] = xn

        @pl.when(pl.program_id(0) >= _P_TILES)
        def _():
            ys_ref[...] = xn
        return
    sc_ref, sh_ref = refs[:2]
    refs = refs[2:]
    u = xn * (1.0 + sc_ref[...]) + sh_ref[...]
    if not route:
        xo_ref, uo_ref = refs
        xo_ref[...] = xn
        uo_ref[...] = u.astype(BF16)
        return
    r_ref, xo_ref, uo_ref, p_ref, i_ref = refs
    xo_ref[...] = xn
    uo_ref[...] = u.astype(BF16)
    r_hi, r_lo = _split2(r_ref[...])
    logits = _dot3(u, r_hi, r_lo)
    lane = lax.broadcasted_iota(jnp.int32, logits.shape, 1)
    l1 = jnp.where(lane < N_EXPERTS, logits, -jnp.inf)
    m1 = jnp.max(l1, axis=-1, keepdims=True)
    i1 = jnp.min(jnp.where(l1 == m1, lane, LANES), axis=-1, keepdims=True)
    l2 = jnp.where(lane == i1, -jnp.inf, l1)
    m2 = jnp.max(l2, axis=-1, keepdims=True)
    i2 = jnp.min(jnp.where(l2 == m2, lane, LANES), axis=-1, keepdims=True)
    e = jnp.exp(m2 - m1)
    p1 = 1.0 / (1.0 + e)
    p2 = e / (1.0 + e)
    p_ref[...] = jnp.where(lane == 0, p1, jnp.where(lane == 1, p2, 0.0))
    i_ref[...] = jnp.where(lane == 0, i1, jnp.where(lane == 1, i2, 0))


def _ln(x, h, mod5, ln_g, ln_b, *, layer, gate_chunk, mod_next=None, router=None, pair_probs=None):
    tm = _LN_TM
    row = pl.BlockSpec((tm, D_MODEL), lambda t: (t, 0))
    vec = pl.BlockSpec((None, 1, D_MODEL), lambda t: (layer, 0, 0))
    lane_row = pl.BlockSpec((tm, LANES), lambda t: (t, 0))
    if pair_probs is None:
        in_specs = [row, row]
        args = [x, h]
    else:
        in_specs = [row, row, row, lane_row]
        args = [x, h[0], h[1], pair_probs]
    in_specs += [_mod_spec(layer, gate_chunk, tm), vec, vec]
    args += [mod5, ln_g.reshape(DEPTH, 1, D_MODEL), ln_b.reshape(DEPTH, 1, D_MODEL)]
    if mod_next is None:
        out_shape = [jax.ShapeDtypeStruct((N_TOK_P, D_MODEL), F32), jax.ShapeDtypeStruct((N_TOK_S, D_MODEL), F32)]
        out_specs = [pl.BlockSpec((tm, D_MODEL), lambda t: (jnp.minimum(t, _P_TILES - 1), 0)),
                     pl.BlockSpec((tm, D_MODEL), lambda t: (jnp.maximum(t - _P_TILES, 0), 0))]
    else:
        nl, sc_chunk, sh_chunk = mod_next
        in_specs += [_mod_spec(nl, sc_chunk, tm), _mod_spec(nl, sh_chunk, tm)]
        args += [mod5, mod5]
        out_shape = [jax.ShapeDtypeStruct((N_TOK, D_MODEL), F32), jax.ShapeDtypeStruct((N_TOK, D_MODEL), BF16)]
        out_specs = [row, row]
    if router is not None:
        in_specs.append(pl.BlockSpec((D_MODEL, LANES), lambda t: (0, 0)))
        args.append(router)
        out_shape += [jax.ShapeDtypeStruct((N_TOK, LANES), F32), jax.ShapeDtypeStruct((N_TOK, LANES), jnp.int32)]
        out_specs += [lane_row, lane_row]
    return pl.pallas_call(
        functools.partial(_ln_body, modulate=mod_next is not None, route=router is not None,
                          gated_pair=pair_probs is not None),
        out_shape=out_shape,
        grid=(N_TOK // tm,),
        in_specs=in_specs,
        out_specs=out_specs,
        compiler_params=_params(("arbitrary",)),
        name="ln",
    )(*args)


def _rms(x, w):
    return x * lax.rsqrt(jnp.mean(x * x, axis=-1, keepdims=True) + EPS) * w


def _split_body(p_ref, qw_ref, kw_ref, qn_ref, ckv_ref, kr_ref):
    q0 = FOURIER_DIM
    k0 = q0 + Q_RANK
    r0 = k0 + KV_RANK
    qn_ref[...] = _rms(p_ref[:, q0:k0], qw_ref[...]).astype(BF16)
    ckv_ref[...] = _rms(p_ref[:, k0:r0], kw_ref[...])
    kr_ref[...] = p_ref[:, r0:]


def _split_proj(proj, q_norm_w, kv_norm_w, layer, tm=512):
    return pl.pallas_call(
        _split_body,
        out_shape=[jax.ShapeDtypeStruct((N_TOK, Q_RANK), BF16),
                   jax.ShapeDtypeStruct((N_TOK, KV_RANK), F32),
                   jax.ShapeDtypeStruct((N_TOK, ROPE_DIM), F32)],
        grid=(N_TOK // tm,),
        in_specs=[pl.BlockSpec((tm, MIX_IN), lambda t: (t, 0)),
                  pl.BlockSpec((None, 1, Q_RANK), lambda t: (layer, 0, 0)),
                  pl.BlockSpec((None, 1, KV_RANK), lambda t: (layer, 0, 0))],
        out_specs=[pl.BlockSpec((tm, Q_RANK), lambda t: (t, 0)),
                   pl.BlockSpec((tm, KV_RANK), lambda t: (t, 0)),
                   pl.BlockSpec((tm, ROPE_DIM), lambda t: (t, 0))],
        compiler_params=_params(("arbitrary",)),
        name="split_proj",
    )(proj, q_norm_w.reshape(N_EVEN, 1, Q_RANK), kv_norm_w.reshape(N_EVEN, 1, KV_RANK))


def _bf16_pair(a):
    hi = a.astype(np.float32).astype(jnp.bfloat16)
    lo = (a - hi.astype(np.float64)).astype(np.float32).astype(jnp.bfloat16)
    return np.stack([hi, lo])


@functools.lru_cache(maxsize=None)
def _dft_tables(length):
    k = np.arange(length)
    ang = 2.0 * np.pi * (np.outer(k, k) % length) / length
    scale = 1.0 / np.sqrt(length * FOURIER_GROUP_DIM)
    t_len = np.concatenate([np.cos(ang), -np.sin(ang)], axis=1) * scale
    c = np.arange(FOURIER_GROUP_DIM)
    angc = 2.0 * np.pi * (np.outer(c, c) % FOURIER_GROUP_DIM) / FOURIER_GROUP_DIM
    eye = np.eye(FOURIER_GROUPS)
    t_ch = np.concatenate([np.kron(eye, np.cos(angc)), np.kron(eye, np.sin(angc))], axis=1)
    return _bf16_pair(t_len), _bf16_pair(t_ch)


def _fourier_body(f_ref, tl_ref, tc_ref, *rest):
    o_ref = rest[-1]
    gh = _dot3(f_ref[...], tc_ref[0], tc_ref[1])
    stacked = jnp.concatenate([gh[:, :FOURIER_DIM], gh[:, FOURIER_DIM:]], axis=0)
    s_hi, s_lo = _split2(stacked)
    y = _dot(tl_ref[0], s_hi) + _dot(tl_ref[1], s_hi) + _dot(tl_ref[0], s_lo)
    o_ref[...] = y.astype(BF16)


def _fourier(proj, n_seq, length, row0, prev=None):
    t_len, t_ch = _dft_tables(length)
    blk0 = row0 // length
    in_specs = [pl.BlockSpec((length, FOURIER_DIM), lambda b: (blk0 + b, 0)),
                pl.BlockSpec((2, length, 2 * length), lambda b: (0, 0, 0)),
                pl.BlockSpec((2, FOURIER_DIM, 2 * FOURIER_DIM), lambda b: (0, 0, 0))]
    args = [proj, jnp.asarray(t_len), jnp.asarray(t_ch)]
    aliases = {}
    if prev is not None:
        in_specs.append(ANY_SPEC)
        args.append(prev)
        aliases = {3: 0}
    return pl.pallas_call(
        _fourier_body,
        out_shape=jax.ShapeDtypeStruct((N_TOK, FOURIER_DIM), BF16),
        grid=(n_seq,),
        in_specs=in_specs,
        out_specs=pl.BlockSpec((length, FOURIER_DIM), lambda b: (blk0 + b, 0)),
        input_output_aliases=aliases,
        compiler_params=_params(("arbitrary",)),
        name="fourier",
    )(*args)


def _rope_pair(x, cos, sin_signed):
    lane = lax.broadcasted_iota(jnp.int32, x.shape, 1)
    first_half = (lane & 31) < 16
    partner = jnp.where(first_half, pltpu.roll(x, LANES - 16, 1), pltpu.roll(x, 16, 1))
    return x * cos + partner * sin_signed


def _attn_body(*refs, rope, ctx):
    q_ref, kv_ref, kr_ref = refs[:3]
    refs = refs[3:]
    if rope:
        cq_ref, sq_ref, ck_ref, sk_ref = refs[:4]
        refs = refs[4:]
    if ctx:
        kvc_ref, krc_ref = refs[:2]
        refs = refs[2:]
    o_ref = refs[-1]

    def half_lanes(x):
        lane = lax.broadcasted_iota(jnp.int32, x.shape, 1)
        return jnp.where(lane < ROPE_DIM, x, 0.0).astype(BF16), jnp.where(lane >= ROPE_DIM, x, 0.0).astype(BF16)

    kr = kr_ref[...]
    kr = jnp.concatenate([kr, kr], axis=1)
    if rope:
        kr = _rope_pair(kr, ck_ref[...], sk_ref[...])
    kr_halves = half_lanes(kr)
    if ctx:
        krc = krc_ref[...]
        krc_halves = half_lanes(jnp.concatenate([krc, krc], axis=1))
    rope0 = MLA_HEADS * NOPE_DIM

    def keys_values(ref, c0, kr_half):
        k = jnp.concatenate([ref[:, c0:c0 + NOPE_DIM], kr_half], axis=1)
        v = ref[:, c0 + NOPE_DIM:c0 + NOPE_DIM + V_DIM]
        return k, jnp.concatenate([v, jnp.ones(v.shape, BF16)], axis=1)

    for pr in range(MLA_HEADS // 2):
        qr = q_ref[:, rope0 + pr * LANES:rope0 + (pr + 1) * LANES]
        if rope:
            qr = _rope_pair(qr, cq_ref[...], sq_ref[...])
        qr_halves = half_lanes(qr * ATTN_SCALE)
        for hh in range(2):
            h = 2 * pr + hh
            qn = (q_ref[:, h * NOPE_DIM:(h + 1) * NOPE_DIM] * ATTN_SCALE).astype(BF16)
            qcat = jnp.concatenate([qn, qr_halves[hh]], axis=1)
            c0 = h * (NOPE_DIM + V_DIM)
            k, v1 = keys_values(kv_ref, c0, kr_halves[hh])
            s = _dot_nt(qcat, k)
            m = jnp.max(s, axis=-1, keepdims=True)
            if ctx:
                kc, vc1 = keys_values(kvc_ref, c0, krc_halves[hh])
                sc = _dot_nt(qcat, kc)
                m = jnp.maximum(m, jnp.max(sc, axis=-1, keepdims=True))
            acc = _dot(jnp.exp(s - m).astype(BF16), v1)
            if ctx:
                acc = acc + _dot(jnp.exp(sc - m).astype(BF16), vc1)
            o_ref[:, h * V_DIM:(h + 1) * V_DIM] = (acc[:, :V_DIM] / acc[:, V_DIM:V_DIM + 1]).astype(BF16)


def _attention(q, kv, kr, *, n_seq, length, row0, tq, rope_tabs=None, ctx=None, prev=None):
    n_qt = length // tq
    qblk0 = row0 // tq
    kblk0 = row0 // length
    kv_w = MLA_HEADS * (NOPE_DIM + V_DIM)
    in_specs = [pl.BlockSpec((tq, MLA_HEADS * QK_DIM), lambda b, i: (qblk0 + b * n_qt + i, 0)),
                pl.BlockSpec((length, kv_w), lambda b, i: (kblk0 + b, 0)),
                pl.BlockSpec((length, ROPE_DIM), lambda b, i: (kblk0 + b, 0))]
    args = [q, kv, kr]
    if rope_tabs is not None:
        cos, sin = rope_tabs
        in_specs += [pl.BlockSpec((tq, LANES), lambda b, i: (i, 0)),
                     pl.BlockSpec((tq, LANES), lambda b, i: (i, 0)),
                     pl.BlockSpec((length, LANES), lambda b, i: (0, 0)),
                     pl.BlockSpec((length, LANES), lambda b, i: (0, 0))]
        args += [cos, sin, cos, sin]
    if ctx is not None:
        krc, cblk0 = ctx
        in_specs += [pl.BlockSpec((PAST_LEN, kv_w), lambda b, i: (cblk0 + b, 0)),
                     pl.BlockSpec((None, PAST_LEN, ROPE_DIM), lambda b, i: (b, 0, 0))]
        args += [kv, krc]
    aliases = {}
    if prev is not None:
        aliases = {len(args): 0}
        in_specs.append(ANY_SPEC)
        args.append(prev)
    return pl.pallas_call(
        functools.partial(_attn_body, rope=rope_tabs is not None, ctx=ctx is not None),
        out_shape=jax.ShapeDtypeStruct((N_TOK, MLA_HEADS * V_DIM), BF16),
        grid=(n_seq, n_qt),
        in_specs=in_specs,
        out_specs=pl.BlockSpec((tq, MLA_HEADS * V_DIM), lambda b, i: (qblk0 + b * n_qt + i, 0)),
        input_output_aliases=aliases,
        compiler_params=_params(("arbitrary", "arbitrary")),
        name="attention",
    )(*args)


def _rope_tables():
    rows = DEC_SEQ // GRID_W
    row = jnp.repeat(jnp.arange(rows, dtype=F32), GRID_W)
    col = jnp.tile(jnp.arange(GRID_W, dtype=F32), rows)
    inv = ROPE_THETA ** (-jnp.arange(ROPE_DIM // 4, dtype=F32) * 2.0 / (ROPE_DIM // 2))
    ang = jnp.stack([row[:, None] * inv, col[:, None] * inv], axis=1)
    cos, sin = jnp.cos(ang), jnp.sin(ang)
    cos64 = jnp.stack([cos, cos], axis=2).reshape(DEC_SEQ, ROPE_DIM)
    sin64 = jnp.stack([-sin, sin], axis=2).reshape(DEC_SEQ, ROPE_DIM)
    return jnp.tile(cos64, (1, 2)), jnp.tile(sin64, (1, 2))


_CONV_PAD = 8
_DIR_HEADS = 2 * HEADS_PER_GROUP


@functools.lru_cache(maxsize=None)
def _head_expand():
    e = np.zeros((_DIR_HEADS, 2 * GROUP_DIM), np.float32)
    for h in range(HEADS_PER_GROUP):
        e[h, h * SSM_HEAD_DIM:(h + 1) * SSM_HEAD_DIM] = 1.0
        e[HEADS_PER_GROUP + h, GROUP_DIM + h * SSM_HEAD_DIM:GROUP_DIM + (h + 1) * SSM_HEAD_DIM] = 1.0
    return e.astype(jnp.bfloat16)


def _scan_body(*refs, length, zero_init, emit_state, n_prev):
    (xr_ref, br_ref, cr_ref, wx_ref, wb_ref, wc_ref, bx_ref, bb_ref, bc_ref,
     dt_ref, bias_ref, acol_ref, dskip_ref, e_ref) = refs[:14]
    refs = refs[14:]
    if not zero_init:
        h0f_ref, h0b_ref = refs[:2]
        refs = refs[2:]
    refs = refs[n_prev:]
    y_ref = refs[0]
    refs = refs[1:]
    if emit_state:
        hf_ref, hb_ref = refs[:2]
        refs = refs[2:]
    padx, padb, padc, y_scr, bm_scr, cm_scr, xwf_scr, xwb_scr, decf_scr, decb_scr, tt_scr = refs
    q = SSM_CHUNK
    n_chunks = length // q
    hpg = HEADS_PER_GROUP
    head0 = pl.multiple_of(pl.program_id(1) * hpg, hpg)
    hd = SSM_HEAD_DIM
    state_shape = (GROUP_DIM, SSM_STATE)

    for raw, pad in ((xr_ref, padx), (br_ref, padb), (cr_ref, padc)):
        zeros = jnp.zeros((_CONV_PAD, pad.shape[1]), F32)
        pad[0:_CONV_PAD, :] = zeros
        pad[_CONV_PAD + length:, :] = zeros
        pad[_CONV_PAD:_CONV_PAD + length, :] = raw[...].astype(F32)

    rr = lax.broadcasted_iota(jnp.int32, (q, q), 0)
    cc = lax.broadcasted_iota(jnp.int32, (q, q), 1)
    lower = cc <= rr
    upper = cc >= rr
    tri_t = upper.astype(BF16)
    pair_lane = lax.broadcasted_iota(jnp.int32, (q, 2 * hd), 1)
    fwd_rows = lax.broadcasted_iota(jnp.int32, (_DIR_HEADS, q), 0) < hpg
    pad_rows = jnp.zeros((q - _DIR_HEADS, q), F32)
    a_neg = -jnp.exp(acol_ref[...])
    bias = bias_ref[...]
    e16 = e_ref[...]

    def conv_silu(pad, w_ref, b_ref, r0):
        n_win = q + 2 * _CONV_PAD
        win = pad[r0:r0 + n_win, :]
        acc = b_ref[...]
        for k in range(SSM_CONV):
            off = _CONV_PAD + k - SSM_CONV // 2
            tap = win[off:off + q, :] if off == _CONV_PAD else pltpu.roll(win, n_win - off, 0)[0:q, :]
            acc = acc + w_ref[k:k + 1, :] * tap
        return _silu(acc)

    totals = []
    for c in range(n_chunks):
        r0 = c * q
        rows = slice(r0, r0 + q)
        x = conv_silu(padx, wx_ref, bx_ref, r0)
        bm = conv_silu(padb, wb_ref, bb_ref, r0).astype(BF16)
        cm = conv_silu(padc, wc_ref, bc_ref, r0).astype(BF16)
        bm_scr[rows, :] = bm
        cm_scr[rows, :] = cm
        tt_scr[c] = (dt_ref[rows, :] + bias).T
        t = jnp.concatenate([tt_scr[c, pl.ds(head0, hpg), :], tt_scr[c, pl.ds(SSM_HEADS + head0, hpg), :]],
                            axis=0)
        dt_t = jnp.maximum(t, 0.0) + jnp.log1p(jnp.exp(-jnp.abs(t)))
        a_t = dt_t * a_neg
        a_hi = a_t.astype(BF16)
        r1 = a_t - a_hi.astype(F32)
        a_mid = r1.astype(BF16)
        a_lo = (r1 - a_mid.astype(F32)).astype(BF16)
        cs_t = _dot(a_hi, tri_t) + _dot(a_mid, tri_t) + _dot(a_lo, tri_t)
        ecs_t = cs_t - a_t
        total = cs_t[:, q - 1:q]
        totals.append(jnp.exp(total))
        cs = jnp.concatenate([cs_t, pad_rows], axis=0).T
        ecs = jnp.concatenate([ecs_t, pad_rows], axis=0).T
        cb = _dot_nt(cm, bm)
        pieces = []
        for pr in range(hpg // 2):
            mats = []
            for h in (2 * pr, 2 * pr + 1):
                lf = jnp.exp(jnp.where(lower, cs[:, h:h + 1] - cs_t[h:h + 1, :], NEG_BIG))
                lb = jnp.exp(jnp.where(upper, ecs_t[hpg + h:hpg + h + 1, :] - ecs[:, hpg + h:hpg + h + 1], NEG_BIG))
                mats.append((cb * (lf * dt_t[h:h + 1, :] + lb * dt_t[hpg + h:hpg + h + 1, :])).astype(BF16))
            xp = x[:, pr * 2 * hd:(pr + 1) * 2 * hd]
            rhs = jnp.concatenate([jnp.where(pair_lane < hd, xp, 0.0), jnp.where(pair_lane >= hd, xp, 0.0)], axis=0)
            pieces.append(_dot(jnp.concatenate(mats, axis=1), rhs.astype(BF16)))
        y_scr[rows, :] = jnp.concatenate(pieces, axis=1) + dskip_ref[...] * x
        dec = _dot_tn(jnp.exp(jnp.where(fwd_rows, cs_t, total - ecs_t)).astype(BF16), e16)
        wgt = _dot_tn((jnp.exp(jnp.where(fwd_rows, total - cs_t, ecs_t)) * dt_t).astype(BF16), e16)
        decf_scr[rows, :] = dec[:, :GROUP_DIM].astype(BF16)
        decb_scr[rows, :] = dec[:, GROUP_DIM:].astype(BF16)
        xwf_scr[rows, :] = (x * wgt[:, :GROUP_DIM]).astype(BF16)
        xwb_scr[rows, :] = (x * wgt[:, GROUP_DIM:]).astype(BF16)

    def step(state, c, dec_scr, xw_scr, row0):
        rows = slice(c * q, (c + 1) * q)
        y_off = _dot_nt(cm_scr[rows, :], state.astype(BF16))
        y_scr[rows, :] = y_scr[rows, :] + y_off * dec_scr[rows, :].astype(F32)
        upd = _dot_tn(xw_scr[rows, :], bm_scr[rows, :])
        decay = totals[c]
        return jnp.concatenate(
            [state[h * hd:(h + 1) * hd, :] * decay[row0 + h:row0 + h + 1, :] + upd[h * hd:(h + 1) * hd, :]
             for h in range(hpg)], axis=0)

    if zero_init:
        hf = jnp.zeros(state_shape, F32)
        hb = jnp.zeros(state_shape, F32)
    else:
        hf = h0f_ref[...].reshape(state_shape)
        hb = h0b_ref[...].reshape(state_shape)
    for i in range(n_chunks):
        hf = step(hf, i, decf_scr, xwf_scr, 0)
        hb = step(hb, n_chunks - 1 - i, decb_scr, xwb_scr, hpg)
    y_ref[...] = y_scr[...].astype(BF16)
    if emit_state:
        hf_ref[...] = hf.reshape(hf_ref.shape)
        hb_ref[...] = hb.reshape(hb_ref.shape)


def _scan(xbc, dt, conv_w, conv_b, dt_bias, alog_col, dskip, layer, *, n_seq, length, row0, h0=None,
          y_prev=None, state_prev=None, emit_state):
    blk0 = row0 // length
    b0 = SSM_INNER // SSM_STATE
    c0 = b0 + SSM_GROUPS
    seq = lambda width, col: pl.BlockSpec((length, width), lambda b, g: (blk0 + b, col + g))
    cw = lambda width, col: pl.BlockSpec((None, SSM_CONV, width), lambda b, g: (layer, 0, col + g))
    cbias = lambda width, col: pl.BlockSpec((None, 1, width), lambda b, g: (layer, 0, col + g))
    grp = lambda width: pl.BlockSpec((None, 1, width), lambda b, g: (layer, 0, g))
    h0_spec = pl.BlockSpec((None, None, HEADS_PER_GROUP, SSM_HEAD_DIM, SSM_STATE), lambda b, g: (b, layer, g, 0, 0))
    in_specs = [seq(GROUP_DIM, 0), seq(SSM_STATE, b0), seq(SSM_STATE, c0),
                cw(GROUP_DIM, 0), cw(SSM_STATE, b0), cw(SSM_STATE, c0),
                cbias(GROUP_DIM, 0), cbias(SSM_STATE, b0), cbias(SSM_STATE, c0),
                pl.BlockSpec((length, 2 * SSM_HEADS), lambda b, g: (blk0 + b, 0)),
                pl.BlockSpec((None, 1, 2 * SSM_HEADS), lambda b, g: (layer, 0, 0)),
                pl.BlockSpec((None, None, _DIR_HEADS, 1), lambda b, g: (layer, g, 0, 0)), grp(GROUP_DIM),
                pl.BlockSpec((_DIR_HEADS, 2 * GROUP_DIM), lambda b, g: (0, 0))]
    cb3 = conv_b.reshape(N_ODD, 1, SSM_CONV_DIM)
    args = [xbc, xbc, xbc, conv_w, conv_w, conv_w, cb3, cb3, cb3, dt, dt_bias, alog_col, dskip,
            jnp.asarray(_head_expand())]
    if h0 is not None:
        in_specs += [h0_spec, h0_spec]
        args += list(h0)
    out_shape = [jax.ShapeDtypeStruct((N_TOK, SSM_INNER), BF16)]
    out_specs = [seq(GROUP_DIM, 0)]
    if emit_state:
        st = jax.ShapeDtypeStruct((n_seq, N_ODD, SSM_HEADS, SSM_HEAD_DIM, SSM_STATE), F32)
        out_shape += [st, st]
        out_specs += [h0_spec, h0_spec]
    aliases = {}
    prevs = ([] if y_prev is None else [y_prev]) + ([] if state_prev is None else list(state_prev))
    for k, prev in enumerate(prevs):
        aliases[len(args)] = k if y_prev is not None else k + 1
        in_specs.append(ANY_SPEC)
        args.append(prev)
    pad_rows = length + 2 * _CONV_PAD
    scratch = [pltpu.VMEM((pad_rows, GROUP_DIM), F32), pltpu.VMEM((pad_rows, SSM_STATE), F32),
               pltpu.VMEM((pad_rows, SSM_STATE), F32), pltpu.VMEM((length, GROUP_DIM), F32),
               pltpu.VMEM((length, SSM_STATE), BF16), pltpu.VMEM((length, SSM_STATE), BF16),
               pltpu.VMEM((length, GROUP_DIM), BF16), pltpu.VMEM((length, GROUP_DIM), BF16),
               pltpu.VMEM((length, GROUP_DIM), BF16), pltpu.VMEM((length, GROUP_DIM), BF16),
               pltpu.VMEM((length // SSM_CHUNK, 2 * SSM_HEADS, SSM_CHUNK), F32)]
    return pl.pallas_call(
        functools.partial(_scan_body, length=length, zero_init=h0 is None, emit_state=emit_state,
                          n_prev=len(prevs)),
        out_shape=out_shape,
        grid=(n_seq, SSM_GROUPS),
        in_specs=in_specs,
        out_specs=out_specs,
        scratch_shapes=scratch,
        input_output_aliases=aliases,
        compiler_params=_params(("arbitrary", "arbitrary")),
        name="ssd_scan",
    )(*args)


def _gate_norm_body(y_ref, z_ref, w_ref, o_ref):
    g = y_ref[...].astype(F32) * _silu(z_ref[...].astype(F32))
    o_ref[...] = _rms(g, w_ref[...]).astype(BF16)


def _gate_norm(y, z, norm_w, layer, tm=512):
    return pl.pallas_call(
        _gate_norm_body,
        out_shape=jax.ShapeDtypeStruct((N_TOK, SSM_INNER), BF16),
        grid=(N_TOK // tm,),
        in_specs=[pl.BlockSpec((tm, SSM_INNER), lambda t: (t, 0)),
                  pl.BlockSpec((tm, SSM_INNER), lambda t: (t, 0)),
                  pl.BlockSpec((None, 1, SSM_INNER), lambda t: (layer, 0, 0))],
        out_specs=pl.BlockSpec((tm, SSM_INNER), lambda t: (t, 0)),
        compiler_params=_params(("arbitrary",)),
        name="gate_norm",
    )(y, z, norm_w.reshape(N_ODD, 1, SSM_INNER))


def _moe_up_body(e_ref, n_ref, m_ref, first_ref, valid_ref, x_ref, wg_ref, wu_ref, o_ref, gb, ub):
    s = pl.program_id(0)

    def emit(rows, wg, wu):
        x = x_ref[rows, :]
        o_ref[rows, :] = (_silu(_dot(x, wg)) * _dot(x, wu)).astype(BF16)

    @pl.when(first_ref[s] == 1)
    def _():
        wg = wg_ref[...].astype(BF16)
        wu = wu_ref[...].astype(BF16)
        gb[...] = wg
        ub[...] = wu
        emit(slice(None), wg, wu)

    @pl.when((first_ref[s] == 0) & (valid_ref[s] == _TILE_FULL))
    def _():
        emit(slice(None), gb[...], ub[...])

    @pl.when((first_ref[s] == 0) & (valid_ref[s] == _TILE_HALF))
    def _():
        emit(slice(0, MOE_TM // 2), gb[...], ub[...])


def _moe_down_body(e_ref, n_ref, m_ref, first_ref, valid_ref, h_ref, w_ref, o_ref, wb):
    s = pl.program_id(0)

    def emit(rows, w):
        o_ref[rows, :] = _dot(h_ref[rows, :], w).astype(BF16)

    @pl.when(first_ref[s] == 1)
    def _():
        w = w_ref[...].astype(BF16)
        wb[...] = w
        emit(slice(None), w)

    @pl.when((first_ref[s] == 0) & (valid_ref[s] == _TILE_FULL))
    def _():
        emit(slice(None), wb[...])

    @pl.when((first_ref[s] == 0) & (valid_ref[s] == _TILE_HALF))
    def _():
        emit(slice(0, MOE_TM // 2), wb[...])


_TILE_SKIP, _TILE_FULL, _TILE_HALF = 0, 1, 2


def _moe_tables(tiles, count, n_col_tiles):
    n_steps = MOE_TILES * n_col_tiles
    tile_start = jnp.cumsum(tiles) - tiles
    step_end = jnp.cumsum(tiles * n_col_tiles)
    total = step_end[-1]
    s = jnp.minimum(jnp.arange(n_steps, dtype=jnp.int32), total - 1)
    e = jnp.sum((s[:, None] >= step_end[None, :]).astype(jnp.int32), axis=1)
    local = s - (step_end - tiles * n_col_tiles)[e]
    te = jnp.maximum(tiles[e], 1)
    n = local // te
    k = local - n * te
    m = tile_start[e] + k
    live = jnp.arange(n_steps) < total
    half = (count[e] - k * MOE_TM) <= MOE_TM // 2
    kind = jnp.where(live, jnp.where(half, _TILE_HALF, _TILE_FULL), _TILE_SKIP).astype(jnp.int32)
    first = ((k == 0) & live).astype(jnp.int32)
    return e.astype(jnp.int32), n.astype(jnp.int32), m.astype(jnp.int32), first, kind


def _moe(u, idx, w_gate, w_up, w_down, layer):
    tm = MOE_TM
    flat_e = idx.reshape(-1)
    onehot = (flat_e[:, None] == jnp.arange(N_EXPERTS)[None, :]).astype(jnp.int32)
    rank = jnp.sum((jnp.cumsum(onehot, axis=0) - onehot) * onehot, axis=1)
    count = jnp.sum(onehot, axis=0)
    tiles = (count + tm - 1) // tm
    row_start = (jnp.cumsum(tiles) - tiles) * tm
    dest = row_start[flat_e] + rank
    token = jnp.arange(N_TOK * TOP_K, dtype=jnp.int32) // TOP_K
    src = jnp.zeros((MOE_ROWS,), jnp.int32).at[dest].set(token, mode="promise_in_bounds", unique_indices=True)
    x_sorted = u.at[src].get(mode="promise_in_bounds")

    tn = 1024
    n_up = EXPERT_DIM // tn
    tabs = _moe_tables(tiles, count, n_up)
    w_spec = pl.BlockSpec((None, None, D_MODEL, tn), lambda s, e, n, m, f, v: (layer, e[s], 0, n[s]))
    hmid = pl.pallas_call(
        _moe_up_body,
        out_shape=jax.ShapeDtypeStruct((MOE_ROWS, EXPERT_DIM), BF16),
        grid_spec=pltpu.PrefetchScalarGridSpec(
            num_scalar_prefetch=5,
            grid=(MOE_TILES * n_up,),
            in_specs=[pl.BlockSpec((tm, D_MODEL), lambda s, e, n, m, f, v: (m[s], 0)), w_spec, w_spec],
            out_specs=pl.BlockSpec((tm, tn), lambda s, e, n, m, f, v: (m[s], n[s])),
            scratch_shapes=[pltpu.VMEM((D_MODEL, tn), BF16), pltpu.VMEM((D_MODEL, tn), BF16)]),
        compiler_params=_params(("arbitrary",)),
        name="moe_up",
    )(*tabs, x_sorted, w_gate, w_up)

    tn = 1024
    n_dn = D_MODEL // tn
    tabs = _moe_tables(tiles, count, n_dn)
    y_sorted = pl.pallas_call(
        _moe_down_body,
        out_shape=jax.ShapeDtypeStruct((MOE_ROWS, D_MODEL), BF16),
        grid_spec=pltpu.PrefetchScalarGridSpec(
            num_scalar_prefetch=5,
            grid=(MOE_TILES * n_dn,),
            in_specs=[pl.BlockSpec((tm, EXPERT_DIM), lambda s, e, n, m, f, v: (m[s], 0)),
                      pl.BlockSpec((None, None, EXPERT_DIM, tn), lambda s, e, n, m, f, v: (layer, e[s], 0, n[s]))],
            out_specs=pl.BlockSpec((tm, tn), lambda s, e, n, m, f, v: (m[s], n[s])),
            scratch_shapes=[pltpu.VMEM((EXPERT_DIM, tn), BF16)]),
        compiler_params=_params(("arbitrary",), vmem_limit=V7X_VMEM_LIMIT_LARGE),
        name="moe_down",
    )(*tabs, hmid, w_down)

    dest2 = dest.reshape(N_TOK, TOP_K)
    return (y_sorted.at[dest2[:, 0]].get(mode="promise_in_bounds"),
            y_sorted.at[dest2[:, 1]].get(mode="promise_in_bounds"))


def _q_perm():
    h = np.arange(MLA_HEADS)[:, None]
    nope = (h * QK_DIM + np.arange(NOPE_DIM)[None, :]).reshape(-1)
    rope = (h * QK_DIM + NOPE_DIM + np.arange(ROPE_DIM)[None, :]).reshape(-1)
    return np.concatenate([nope, rope])


def _even_mixer(u, j, p, cache_ckv, cache_krope, rope_tabs):
    proj = _matmul([u], [p['mix_w_in']], layer=j, n_cols=MIX_IN, tn=MIX_IN, tm=512, out_dtype=F32, name="mix_in")
    qn, ckv, kr = _split_proj(proj, p['q_norm_w'], p['kv_norm_w'], j)
    f = _fourier(proj, BATCH, SEQ, 0)
    f = _fourier(proj, DEC_BATCH, DEC_SEQ, N_TOK_P, prev=f)
    q = _matmul([qn], [p['w_q_perm']], layer=j, n_cols=MLA_HEADS * QK_DIM, tn=MLA_HEADS * QK_DIM, tm=1024,
                out_dtype=F32, name="q_proj")
    ckv_all = jnp.concatenate([ckv, cache_ckv[:, j].reshape(DEC_BATCH * PAST_LEN, KV_RANK)], axis=0).astype(BF16)
    kv = _matmul([ckv_all], [p['w_kv_b']], layer=j, n_cols=MLA_HEADS * (NOPE_DIM + V_DIM),
                 tn=MLA_HEADS * (NOPE_DIM + V_DIM), tm=1024, out_dtype=BF16, name="kv_proj")
    o = _attention(q, kv, kr, n_seq=BATCH, length=SEQ, row0=0, tq=SEQ)
    o = _attention(q, kv, kr, n_seq=DEC_BATCH, length=DEC_SEQ, row0=N_TOK_P, tq=256, rope_tabs=rope_tabs,
                   ctx=(cache_krope[:, j], N_TOK // PAST_LEN), prev=o)
    h = _matmul([f, o], [p['mix_w_out']], layer=j, n_cols=D_MODEL, tn=1024, tm=1024, out_dtype=BF16, name="mix_out")
    return h, ckv, kr


def _group_head_rows(w):
    w = w.reshape(N_ODD, 2, SSM_GROUPS, HEADS_PER_GROUP)
    return jnp.moveaxis(w, 1, 2).reshape(N_ODD, SSM_GROUPS, _DIR_HEADS, 1)


def _odd_mixer(u, j, p, state_f, state_b, state_prev):
    z = _matmul([u], [p['ssm_w_in']], layer=j, n_cols=SSM_INNER, tn=1024, tm=1024, out_dtype=BF16, name="ssm_in_z")
    xbc = _matmul([u], [p['ssm_w_in']], layer=j, n_cols=SSM_CONV_DIM, col0=SSM_INNER, tn=1024, tm=1024,
                  out_dtype=BF16, name="ssm_in_xbc")
    dt = _matmul([u], [p['ssm_w_in']], layer=j, n_cols=2 * SSM_HEADS, col0=SSM_INNER + SSM_CONV_DIM, tn=2 * SSM_HEADS,
                 tm=1024, out_dtype=F32, name="ssm_dt")
    common = (xbc, dt, p['ssm_conv_w'], p['ssm_conv_b'], p['ssm_dt_bias'], p['ssm_alog_col'], p['ssm_dskip'], j)
    y, hf, hb = _scan(*common, n_seq=BATCH, length=SEQ, row0=0, state_prev=state_prev, emit_state=True)
    (y,) = _scan(*common, n_seq=DEC_BATCH, length=DEC_SEQ, row0=N_TOK_P, h0=(state_f, state_b), y_prev=y,
                 emit_state=False)
    yn = _gate_norm(y, z, p['ssm_norm_w'], j)
    h = _matmul([yn], [p['ssm_w_out']], layer=j, n_cols=D_MODEL, tn=512, tm=1024, out_dtype=BF16, name="ssm_out")
    return h, hf, hb


def kernel(x_prompt, x_sample, cache_ckv, cache_krope, state_ssm_fwd, state_ssm_bwd, c, c_ctx, ada_w, ada_b, ln1_g, ln1_b, ln2_g, ln2_b, mix_w_in, q_norm_w, w_q_b, kv_norm_w, w_kv_b, mix_w_out, ffn_w_gate, ffn_w_up, ffn_w_down, ssm_w_in, ssm_conv_w, ssm_conv_b, ssm_dt_bias, ssm_a_log, ssm_d, ssm_norm_w, ssm_w_out, moe_router, moe_w_gate, moe_w_up, moe_w_down):
    p = dict(mix_w_in=mix_w_in, q_norm_w=q_norm_w, w_q_perm=w_q_b[:, :, _q_perm()], kv_norm_w=kv_norm_w,
             w_kv_b=w_kv_b, mix_w_out=mix_w_out, ssm_w_in=ssm_w_in, ssm_conv_w=ssm_conv_w, ssm_conv_b=ssm_conv_b,
             ssm_dt_bias=ssm_dt_bias.reshape(N_ODD, 1, 2 * SSM_HEADS),
             ssm_alog_col=_group_head_rows(ssm_a_log),
             ssm_dskip=jnp.repeat(ssm_d, SSM_HEAD_DIM, axis=1).reshape(N_ODD, 1, SSM_INNER),
             ssm_norm_w=ssm_norm_w, ssm_w_out=ssm_w_out)
    cond = jnp.concatenate([c_ctx[None, :], c, jnp.zeros((COND_ROWS - N_COND, D_MODEL), F32)], axis=0)
    mod = _ada(cond, ada_w, ada_b)
    mod5 = mod[:, :N_COND].reshape(DEPTH, N_COND, N_MOD, 1, D_MODEL)
    rope_tabs = _rope_tables()

    x, u = _modulate(x_prompt, x_sample, mod5, 0)
    ckvs, krs = [], []
    states = None
    for i in range(DEPTH):
        j = i // 2
        if i % 2 == 0:
            h, ckv, kr = _even_mixer(u, j, p, cache_ckv, cache_krope, rope_tabs)
            ckvs.append(ckv[:N_TOK_P].reshape(BATCH, SEQ, KV_RANK))
            krs.append(kr[:N_TOK_P].reshape(BATCH, SEQ, ROPE_DIM))
            x, u = _ln(x, h, mod5, ln1_g, ln1_b, layer=i, gate_chunk=2, mod_next=(i, 4, 3))
            hmid = _matmul([u], [ffn_w_gate, ffn_w_up], layer=j, n_cols=FFN_DIM, tn=512, tm=1024, out_dtype=BF16,
                           swiglu=True, name="ffn_up")
            f = _matmul([hmid], [ffn_w_down], layer=j, n_cols=D_MODEL, tn=512, tm=512, out_dtype=BF16,
                        name="ffn_down")
            pair_probs = None
        else:
            h, hf, hb = _odd_mixer(u, j, p, state_ssm_fwd, state_ssm_bwd, states)
            states = (hf, hb)
            router = jnp.pad(moe_router[j], ((0, 0), (0, LANES - N_EXPERTS)))
            x, u, pair_probs, idx = _ln(x, h, mod5, ln1_g, ln1_b, layer=i, gate_chunk=2, mod_next=(i, 4, 3),
                                        router=router)
            f = _moe(u, idx[:, :TOP_K], moe_w_gate, moe_w_up, moe_w_down, j)
        if i + 1 < DEPTH:
            x, u = _ln(x, f, mod5, ln2_g, ln2_b, layer=i, gate_chunk=5, mod_next=(i + 1, 1, 0),
                       pair_probs=pair_probs)
        else:
            y_prompt, y_sample = _ln(x, f, mod5, ln2_g, ln2_b, layer=i, gate_chunk=5, pair_probs=pair_probs)
    return (y_prompt.reshape(BATCH, SEQ, D_MODEL), y_sample.reshape(DEC_BATCH, DEC_SEQ, D_MODEL),
            jnp.stack(ckvs, axis=1), jnp.stack(krs, axis=1), states[0], states[1])
```
